```python
import math
import jax, jax.numpy as jnp
from jax import lax
import numpy as np

D_MODEL = 1024
BATCH = 1
SEQ = 16384
DEPTH = 2

D_RWKV = D_MODEL // 2
RWKV_HEAD = 64
RWKV_HEADS = D_RWKV // RWKV_HEAD
D_CONV = D_MODEL - D_RWKV
CONV_WIDTH = 3
LORA_W = 64
LORA_A = 64
LORA_G = 128
D_IN_A = 3 * D_RWKV + LORA_W + LORA_A + LORA_G
D_IN_B = 3 * D_CONV
D_IN = D_IN_A + D_IN_B
GN_EPS = 64e-5
SSM_GROUP = 16
SSM_GROUPS = D_MODEL // SSM_GROUP
SSM_STATE = 64
SSM_CHUNK = 128
D_FF = 4 * D_MODEL
D_PLE = 256
RMS_EPS = 1e-6

kernel_name = 'hybrid_rwkv7_shortconv_s5_block'


def rms_norm(x, g):
    xf = x.astype(jnp.float32)
    y = xf * lax.rsqrt(jnp.mean(xf * xf, axis=-1, keepdims=True) + RMS_EPS)
    return (y * g.astype(jnp.float32)).astype(x.dtype)


def shift_prev(x, n=1):
    return jnp.pad(x, ((0, 0), (n, 0), (0, 0)))[:, :x.shape[1]]


def rwkv7_recurrence(r, w, k, v, a, b):
    bsz, _, heads, n = r.shape

    def step(s, inp):
        r_t, w_t, k_t, v_t, a_t, b_t = inp
        sa = jnp.einsum('bhij,bhj->bhi', s, a_t)
        s = s * w_t[:, :, None, :] + sa[..., :, None] * b_t[..., None, :] + v_t[..., :, None] * k_t[..., None, :]
        return s, jnp.einsum('bhij,bhj->bhi', s, r_t)

    xs = tuple(jnp.moveaxis(t, 1, 0) for t in (r, w, k, v, a, b))
    s0 = jnp.zeros((bsz, heads, n, n), jnp.float32)
    _, y = lax.scan(step, s0, xs)
    return jnp.moveaxis(y, 0, 1)


def rwkv_conv_mixer(hn, w_in, shift_mu, w0, w_lora_up, a0, a_lora_up, g_lora_up,
                    k_k, k_a, r_k, ln_w, ln_b, conv_w, w_out):
    bsz, t, _ = hn.shape
    f32 = jnp.float32
    z = hn @ w_in
    za, zb = z[..., :D_IN_A], z[..., D_IN_A:]
    za = za + shift_mu * (shift_prev(za) - za)
    r, k, v, xw, xa, xg = jnp.split(
        za, [D_RWKV, 2 * D_RWKV, 3 * D_RWKV, 3 * D_RWKV + LORA_W, 3 * D_RWKV + LORA_W + LORA_A], axis=-1)
    w_log = -jax.nn.softplus(-(w0 + jnp.tanh(xw) @ w_lora_up)) - 0.5
    decay = jnp.exp(-jnp.exp(w_log.astype(f32)))
    a = jax.nn.sigmoid(a0 + xa @ a_lora_up)
    g = jax.nn.sigmoid(xg) @ g_lora_up

    def heads(u):
        return u.reshape(bsz, t, RWKV_HEADS, RWKV_HEAD).astype(f32)

    kk = heads(k * k_k)
    kk = kk / jnp.maximum(jnp.sqrt(jnp.sum(kk * kk, axis=-1, keepdims=True)), 1e-12)
    k = k * (1.0 + (a - 1.0) * k_a)
    r_h, k_h, v_h, a_h = heads(r), heads(k), heads(v), heads(a)
    y = rwkv7_recurrence(r_h, heads(decay), k_h, v_h, -kk, kk * a_h)
    mu = jnp.mean(y, axis=-1, keepdims=True)
    var = jnp.mean(jnp.square(y - mu), axis=-1, keepdims=True)
    y = (y - mu) * lax.rsqrt(var + GN_EPS)
    y = y * ln_w.reshape(RWKV_HEADS, RWKV_HEAD).astype(f32) + ln_b.reshape(RWKV_HEADS, RWKV_HEAD).astype(f32)
    bonus = jnp.sum(r_h * k_h * r_k.astype(f32), axis=-1, keepdims=True) * v_h
    y_a = ((y + bonus).reshape(bsz, t, D_RWKV) * g.astype(f32)).astype(hn.dtype)
    b_gate, c_gate, xin = jnp.split(zb, 3, axis=-1)
    u = c_gate * xin
    conv = conv_w[0] * u + conv_w[1] * shift_prev(u, 1) + conv_w[2] * shift_prev(u, 2)
    y_b = b_gate * conv
    return jnp.concatenate([y_a, y_b], axis=-1) @ w_out


def _affine_combine(e1, e2):
    a1, b1 = e1
    a2, b2 = e2
    return a2 * a1, a2 * b1 + b2


def s5_mixer(hn, lambda_re, lambda_im, log_step, b_re, b_im, c_re, c_im, d_skip, glu_w1, glu_w2):
    bsz, t, _ = hn.shape
    f32, c64 = jnp.float32, jnp.complex64
    u = hn.astype(f32).reshape(bsz, t, SSM_GROUPS, SSM_GROUP)
    lam = lax.complex(jnp.minimum(lambda_re.astype(f32), -1e-4), lambda_im.astype(f32))
    step = jnp.exp(log_step.astype(f32))[:, None]
    lam_bar = jnp.exp(lam * step)
    b_bar = ((lam_bar - 1.0) / lam)[..., None] * lax.complex(b_re.astype(f32), b_im.astype(f32))
    c = lax.complex(c_re.astype(f32), c_im.astype(f32))
    n_chunks = t // SSM_CHUNK
    u_chunks = u.reshape(bsz, n_chunks, SSM_CHUNK, SSM_GROUPS, SSM_GROUP).transpose(1, 2, 0, 3, 4)

    def chunk_step(h_prev, u_c):
        bu = jnp.einsum('gpc,lbgc->lbgp', b_bar, u_c.astype(c64))
        a_cum, h_loc = lax.associative_scan(_affine_combine, (jnp.broadcast_to(lam_bar, bu.shape), bu), axis=0)
        states = h_loc + a_cum * h_prev[None]
        y_c = jnp.real(jnp.einsum('gcp,lbgp->lbgc', c, states))
        return states[-1], y_c

    h0 = jnp.zeros((bsz, SSM_GROUPS, SSM_STATE), c64)
    _, y = lax.scan(chunk_step, h0, u_chunks)
    y = y.transpose(2, 0, 1, 3, 4).reshape(bsz, t, SSM_GROUPS, SSM_GROUP)
    y = (y + d_skip.reshape(SSM_GROUPS, SSM_GROUP).astype(f32) * u).reshape(bsz, t, D_MODEL)
    y = jax.nn.gelu(y).astype(hn.dtype)
    return (y @ glu_w1) * jax.nn.sigmoid(y @ glu_w2)


def sqrelu_mlp(hn, ffn_up, ffn_down):
    return jnp.square(jax.nn.relu(hn @ ffn_up)) @ ffn_down


def setup_inputs(seed: int = 0) -> dict:
    key = jax.random.key(seed)
    ks = iter(jax.random.split(key, 64))

    def nrm(shape, scale):
        return jax.random.normal(next(ks), shape, jnp.float32) * scale

    def gain(n):
        return 1.0 + nrm((n,), 0.02)

    d = D_MODEL
    ratio = jnp.arange(D_RWKV, dtype=jnp.float32) / (D_RWKV - 1)
    inp = {}
    inp['x'] = nrm((BATCH, SEQ, d), 1.0)
    inp['p'] = nrm((DEPTH, BATCH, SEQ, D_PLE), 1.0)
    inp['l0_norm_mix'] = gain(d)
    inp['l0_w_in'] = nrm((d, D_IN), d ** -0.5)
    inp['l0_shift_mu'] = jax.random.uniform(next(ks), (D_IN_A,), jnp.float32)
    inp['l0_w0'] = -6.5 + 5.0 * ratio ** 0.85 + nrm((D_RWKV,), 0.1)
    inp['l0_w_lora_up'] = nrm((LORA_W, D_RWKV), 0.1 * LORA_W ** -0.5)
    inp['l0_a0'] = nrm((D_RWKV,), 0.1)
    inp['l0_a_lora_up'] = nrm((LORA_A, D_RWKV), 0.1 * LORA_A ** -0.5)
    inp['l0_g_lora_up'] = nrm((LORA_G, D_RWKV), LORA_G ** -0.5)
    inp['l0_k_k'] = 0.85 + nrm((D_RWKV,), 0.02)
    inp['l0_k_a'] = 1.0 + nrm((D_RWKV,), 0.02)
    inp['l0_r_k'] = nrm((RWKV_HEADS, RWKV_HEAD), 0.1)
    inp['l0_ln_w'] = gain(D_RWKV)
    inp['l0_ln_b'] = nrm((D_RWKV,), 0.02)
    inp['l0_conv_w'] = nrm((CONV_WIDTH, D_CONV), CONV_WIDTH ** -0.5)
    inp['l0_w_out'] = nrm((d, d), d ** -0.5)
    inp['l0_norm_ffn'] = gain(d)
    inp['l0_ffn_up'] = nrm((d, D_FF), d ** -0.5)
    inp['l0_ffn_down'] = nrm((D_FF, d), D_FF ** -0.5)
    inp['l0_norm_ple'] = gain(d)
    inp['l0_ple_gate'] = nrm((d, d), d ** -0.5)
    inp['l0_ple_proj'] = nrm((D_PLE, d), D_PLE ** -0.5)
    inp['l1_norm_mix'] = gain(d)
    inp['l1_lambda_re'] = -0.5 + nrm((SSM_GROUPS, SSM_STATE), 0.01)
    inp['l1_lambda_im'] = math.pi * jnp.arange(SSM_STATE, dtype=jnp.float32)[None, :] + nrm((SSM_GROUPS, SSM_STATE), 0.01)
    inp['l1_log_step'] = jax.random.uniform(next(ks), (SSM_GROUPS,), jnp.float32, math.log(1e-3), math.log(1e-1))
    inp['l1_b_re'] = nrm((SSM_GROUPS, SSM_STATE, SSM_GROUP), (2 * SSM_GROUP) ** -0.5)
    inp['l1_b_im'] = nrm((SSM_GROUPS, SSM_STATE, SSM_GROUP), (2 * SSM_GROUP) ** -0.5)
    inp['l1_c_re'] = nrm((SSM_GROUPS, SSM_GROUP, SSM_STATE), SSM_STATE ** -0.5)
    inp['l1_c_im'] = nrm((SSM_GROUPS, SSM_GROUP, SSM_STATE), SSM_STATE ** -0.5)
    inp['l1_d_skip'] = nrm((d,), 1.0)
    inp['l1_glu_w1'] = nrm((d, d), d ** -0.5)
    inp['l1_glu_w2'] = nrm((d, d), d ** -0.5)
    inp['l1_norm_ffn'] = gain(d)
    inp['l1_ffn_up'] = nrm((d, D_FF), d ** -0.5)
    inp['l1_ffn_down'] = nrm((D_FF, d), D_FF ** -0.5)
    inp['l1_norm_ple'] = gain(d)
    inp['l1_ple_gate'] = nrm((d, d), d ** -0.5)
    inp['l1_ple_proj'] = nrm((D_PLE, d), D_PLE ** -0.5)
    inp['norm_final'] = gain(d)
    return inp


def reference(x, p,
              l0_norm_mix, l0_w_in, l0_shift_mu, l0_w0, l0_w_lora_up, l0_a0, l0_a_lora_up, l0_g_lora_up,
              l0_k_k, l0_k_a, l0_r_k, l0_ln_w, l0_ln_b, l0_conv_w, l0_w_out,
              l0_norm_ffn, l0_ffn_up, l0_ffn_down, l0_norm_ple, l0_ple_gate, l0_ple_proj,
              l1_norm_mix, l1_lambda_re, l1_lambda_im, l1_log_step, l1_b_re, l1_b_im, l1_c_re, l1_c_im,
              l1_d_skip, l1_glu_w1, l1_glu_w2,
              l1_norm_ffn, l1_ffn_up, l1_ffn_down, l1_norm_ple, l1_ple_gate, l1_ple_proj,
              norm_final):
    mix_fns = [rwkv_conv_mixer, s5_mixer]
    mix_params = [
        (l0_w_in, l0_shift_mu, l0_w0, l0_w_lora_up, l0_a0, l0_a_lora_up, l0_g_lora_up,
         l0_k_k, l0_k_a, l0_r_k, l0_ln_w, l0_ln_b, l0_conv_w, l0_w_out),
        (l1_lambda_re, l1_lambda_im, l1_log_step, l1_b_re, l1_b_im, l1_c_re, l1_c_im,
         l1_d_skip, l1_glu_w1, l1_glu_w2),
    ]
    norm_mix = [l0_norm_mix, l1_norm_mix]
    norm_ffn = [l0_norm_ffn, l1_norm_ffn]
    ffn = [(l0_ffn_up, l0_ffn_down), (l1_ffn_up, l1_ffn_down)]
    norm_ple = [l0_norm_ple, l1_norm_ple]
    ple = [(l0_ple_gate, l0_ple_proj), (l1_ple_gate, l1_ple_proj)]

    h = x
    for i in range(DEPTH):
        h = h + mix_fns[i % 2](rms_norm(h, norm_mix[i]), *mix_params[i])
        h = h + sqrelu_mlp(rms_norm(h, norm_ffn[i]), *ffn[i])
        gate = jax.nn.sigmoid(rms_norm(h, norm_ple[i]) @ ple[i][0])
        h = h + (p[i] @ ple[i][1]) * gate
    return rms_norm(h, norm_final)
```

```python
import functools

import numpy as np
import jax
import jax.numpy as jnp
from jax import lax
from jax.experimental import pallas as pl
from jax.experimental.pallas import tpu as pltpu

F32 = jnp.float32
BF16 = jnp.bfloat16

D_MODEL = 1024
N_HEADS = 8
HEAD = 64
D_RWKV = N_HEADS * HEAD
D_CONV = D_MODEL - D_RWKV
LORA_W, LORA_A, LORA_G = 64, 64, 128
D_LORA = LORA_W + LORA_A + LORA_G
D_IN_A = 3 * D_RWKV + D_LORA
GN_EPS = 64e-5
RMS_EPS = 1e-6
SSM_GROUPS, SSM_GROUP, SSM_STATE = 64, 16, 64
D_FF = 4 * D_MODEL
D_PLE = 256

LANES = 128
VMEM_LIMIT = 56 * 1024 * 1024
HIGHEST = lax.Precision.HIGHEST

W_KK = 2 * D_RWKV
W_KR = 2 * D_RWKV
OFF_KR = W_KK
OFF_V = OFF_KR + W_KR
OFF_LORA = OFF_V + D_RWKV
W_ZA = OFF_LORA + D_LORA
W_Z = W_ZA + 3 * D_CONV

RWKV_BLOCK = 32
K1_ROWS = 256
K2_ROWS = 512
TAIL_ROWS = 512
FF_CHUNK = 1024
S5_BLOCK = 16


def _dot(a, b):
    return jnp.dot(a, b, preferred_element_type=F32)


def _split_bf16(x, n):
    pieces = []
    for _ in range(n - 1):
        p = x.astype(BF16)
        pieces.append(p)
        x = x - p.astype(F32)
    pieces.append(x.astype(BF16))
    return pieces


def _dot_wide_lhs(x, w, n):
    return sum(_dot(p, w) for p in _split_bf16(x, n))


def _dot_wide_rhs(w, x, n):
    return sum(_dot(w, p) for p in _split_bf16(x, n))


def _rms(x, gain):
    return x * lax.rsqrt(jnp.mean(x * x, axis=-1, keepdims=True) + RMS_EPS) * gain


def _sigmoid(x):
    return 1.0 / (1.0 + jnp.exp(-x))


def _softplus(x):
    return jnp.maximum(x, 0.0) + jnp.log(1.0 + jnp.exp(-jnp.abs(x)))


def _gelu_tanh(x):
    return 0.5 * x * (1.0 + jnp.tanh(np.sqrt(2.0 / np.pi).astype(np.float32) * (x + 0.044715 * (x * x * x))))


def _const_spec(shape):
    nd = len(shape)
    return pl.BlockSpec(shape, lambda *_: (0,) * nd, pipeline_mode=pl.Buffered(1))


def _row_spec(rows, cols):
    return pl.BlockSpec((rows, cols), lambda i: (i, 0))


def _params(*sem):
    return pltpu.CompilerParams(dimension_semantics=sem, vmem_limit_bytes=VMEM_LIMIT)


def _k1_kernel(x_ref, gmix_ref, win_ref, mu_ref, w0_ref, ww_ref, a0_ref, wa_ref, wg_ref, kk_ref, ka_ref,
               rk_ref, cw_ref, tri_ref, sel_ref, rnorm_ref, rbonus_ref,
               ab_ref, kr_ref, v_ref, gam_ref, g_ref, bv_ref, yb_ref, zs_ref, us_ref):
    rows = x_ref.shape[0]

    @pl.when(pl.program_id(0) == 0)
    def _():
        zs_ref[0:8, :] = jnp.zeros((8, W_ZA), F32)
        us_ref[0:8, :] = jnp.zeros((8, D_CONV), F32)

    xn = _rms(x_ref[...], gmix_ref[...])
    z = _dot(xn.astype(BF16), win_ref[...])

    za = z[:, :W_ZA]
    zs_ref[8:8 + rows, :] = za
    za_prev = zs_ref[7:7 + rows, :]
    zs_ref[7:8, :] = za[rows - 1:rows, :]
    za = za + mu_ref[...] * (za_prev - za)

    kdup = za[:, :W_KK]
    krr = za[:, OFF_KR:OFF_KR + W_KR]
    v = za[:, OFF_V:OFF_V + D_RWKV]
    lora_in = za[:, OFF_LORA:W_ZA]

    w_log = -_softplus(-(w0_ref[...] + _dot(jnp.tanh(lora_in).astype(BF16), ww_ref[...]))) - 0.5
    logw = -jnp.exp(w_log)
    lr = _sigmoid(a0_ref[...] + _dot(lora_in.astype(BF16), wa_ref[...]))
    gate = _dot(_sigmoid(lora_in).astype(BF16), wg_ref[...])

    c_incl = _dot_wide_rhs(tri_ref[...], logw, 3)
    c_prev = c_incl - logw
    e_neg = jnp.exp(-c_incl)
    slot0 = (lax.broadcasted_iota(jnp.int32, (1, W_KK), 1) // HEAD) % 2 == 0

    kk = kdup * kk_ref[...]
    nrm2 = _dot_wide_lhs(kk * kk, rnorm_ref[...], 2)
    kk = kk / jnp.maximum(jnp.sqrt(nrm2), 1e-12)
    ab_ref[...] = jnp.where(slot0, -kk * jnp.exp(c_prev), kk * lr * e_neg)

    kr = jnp.where(slot0, krr * (1.0 + (lr - 1.0) * ka_ref[...]), krr)
    prod = kr * pltpu.roll(kr, HEAD, axis=1) * rk_ref[...]
    bonus = _dot_wide_lhs(prod, rbonus_ref[...], 2)
    kr_ref[...] = kr * jnp.where(slot0, e_neg, jnp.exp(c_incl))
    v_ref[...] = v
    bv_ref[...] = bonus * v
    g_ref[...] = gate
    gam_ref[...] = jnp.exp(_dot_wide_rhs(sel_ref[...], logw, 3))

    zb = z[:, W_ZA:]
    b_gate = zb[:, :D_CONV]
    u = zb[:, D_CONV:2 * D_CONV] * zb[:, 2 * D_CONV:]
    us_ref[8:8 + rows, :] = u
    u1 = us_ref[7:7 + rows, :]
    u2 = us_ref[6:6 + rows, :]
    us_ref[6:8, :] = u[rows - 2:rows, :]
    cw = cw_ref[...]
    yb_ref[...] = (b_gate * (cw[0:1] * u + cw[1:2] * u1 + cw[2:3] * u2)).astype(BF16)


def _k1(x2d, gmix, win2, mu2, w0_2, ww, a0_2, wa, wg, kk2, ka2, rk2, conv_w):
    t = x2d.shape[0]
    rows = K1_ROWS
    nblk = rows // RWKV_BLOCK
    r_idx = np.arange(rows)
    same = (r_idx[:, None] // RWKV_BLOCK) == (r_idx[None, :] // RWKV_BLOCK)
    tri = jnp.asarray(same & (r_idx[None, :] <= r_idx[:, None]), BF16)
    sel = jnp.asarray(np.arange(nblk)[:, None] == (r_idx[None, :] // RWKV_BLOCK), BF16)
    lane = np.arange(W_KK)
    rnorm = jnp.asarray(0.5 * ((lane[:, None] // (2 * HEAD)) == (lane[None, :] // (2 * HEAD))), BF16)
    rbonus = jnp.asarray((lane[:, None] // (2 * HEAD)) == (np.arange(D_RWKV)[None, :] // HEAD), BF16)

    consts = [gmix, win2, mu2, w0_2, ww, a0_2, wa, wg, kk2, ka2, rk2, conv_w, tri, sel, rnorm, rbonus]
    out_shape = [
        jax.ShapeDtypeStruct((t, W_KK), F32),
        jax.ShapeDtypeStruct((t, W_KR), F32),
        jax.ShapeDtypeStruct((t, D_RWKV), F32),
        jax.ShapeDtypeStruct((t // RWKV_BLOCK, W_KK), F32),
        jax.ShapeDtypeStruct((t, D_RWKV), F32),
        jax.ShapeDtypeStruct((t, D_RWKV), F32),
        jax.ShapeDtypeStruct((t, D_CONV), BF16),
    ]
    out_specs = [_row_spec(rows, W_KK), _row_spec(rows, W_KR), _row_spec(rows, D_RWKV),
                 _row_spec(nblk, W_KK), _row_spec(rows, D_RWKV), _row_spec(rows, D_RWKV),
                 _row_spec(rows, D_CONV)]
    return pl.pallas_call(
        _k1_kernel,
        grid=(t // rows,),
        in_specs=[_row_spec(rows, D_MODEL)] + [_const_spec(c.shape) for c in consts],
        out_specs=out_specs,
        out_shape=out_shape,
        scratch_shapes=[pltpu.VMEM((rows + 8, W_ZA), F32), pltpu.VMEM((rows + 8, D_CONV), F32)],
        compiler_params=_params("arbitrary"),
        name="k1_inproj_prep",
    )(x2d, *consts)


def _columns(tile, r):
    r0 = jnp.broadcast_to(tile[r:r + 1, 0:LANES], (HEAD, LANES))
    r1 = jnp.broadcast_to(tile[r:r + 1, LANES:2 * LANES], (HEAD, LANES))
    m = jnp.concatenate([r0, r1], axis=0).T
    return m[0:HEAD], m[HEAD:2 * HEAD]


def _k2_kernel(ab_ref, kr_ref, v_ref, gam_ref, y_ref, s_ref):
    rows = ab_ref.shape[0]
    sub = 8
    tiles_per_block = RWKV_BLOCK // sub
    row_id = lax.broadcasted_iota(jnp.int32, (sub, LANES), 0)

    @pl.when(pl.program_id(1) == 0)
    def _():
        s_ref[...] = jnp.zeros_like(s_ref)

    def tile_steps(ti, s):
        base = pl.multiple_of(ti * sub, sub)
        ab_t, kr_t, v_t = ab_ref[pl.ds(base, sub), :], kr_ref[pl.ds(base, sub), :], v_ref[pl.ds(base, sub), :]
        y_t = jnp.zeros((sub, LANES), F32)
        for r in range(sub):
            a_col, b_col = _columns(ab_t, r)
            k_col, r_col = _columns(kr_t, r)
            sa = jnp.sum(s * a_col, axis=0, keepdims=True)
            s = s + sa * b_col + v_t[r:r + 1, :] * k_col
            y_t = jnp.where(row_id == r, jnp.sum(s * r_col, axis=0, keepdims=True), y_t)
        y_ref[pl.ds(base, sub), :] = y_t
        return s

    def block(bi, s):
        s = lax.fori_loop(bi * tiles_per_block, (bi + 1) * tiles_per_block, tile_steps, s)
        gbase = pl.multiple_of((bi // sub) * sub, sub)
        g_t = gam_ref[pl.ds(gbase, sub), :]
        g_row = jnp.sum(jnp.where(lax.broadcasted_iota(jnp.int32, g_t.shape, 0) == bi % sub, g_t, 0.0),
                        axis=0, keepdims=True)
        g_col, _ = _columns(g_row, 0)
        return s * g_col

    s_ref[...] = lax.fori_loop(0, rows // RWKV_BLOCK, block, s_ref[...])


def _k2(ab, kr, v, gam):
    t = ab.shape[0]
    rows = K2_ROWS
    pair = lambda p, i: (i, p)
    return pl.pallas_call(
        _k2_kernel,
        grid=(N_HEADS // 2, t // rows),
        in_specs=[pl.BlockSpec((rows, 2 * LANES), pair), pl.BlockSpec((rows, 2 * LANES), pair),
                  pl.BlockSpec((rows, LANES), pair), pl.BlockSpec((rows // RWKV_BLOCK, 2 * LANES), pair)],
        out_specs=pl.BlockSpec((rows, LANES), pair),
        out_shape=jax.ShapeDtypeStruct((t, D_RWKV), F32),
        scratch_shapes=[pltpu.VMEM((HEAD, LANES), F32)],
        compiler_params=_params("arbitrary", "arbitrary"),
        name="k2_rwkv_scan",
    )(ab, kr, v, gam)


def _tail(h, p_ref, nffn_ref, up_ref, down_ref, nple_ref, gate_ref, proj_ref):
    hn = _rms(h, nffn_ref[...]).astype(BF16)
    mlp = None
    for c in range(D_FF // FF_CHUNK):
        a = jnp.maximum(_dot(hn, up_ref[:, c * FF_CHUNK:(c + 1) * FF_CHUNK]), 0.0)
        part = _dot((a * a).astype(BF16), down_ref[c * FF_CHUNK:(c + 1) * FF_CHUNK, :])
        mlp = part if mlp is None else mlp + part
    h = h + mlp
    gate = _sigmoid(_dot(_rms(h, nple_ref[...]).astype(BF16), gate_ref[...]))
    return h + _dot(p_ref[...].astype(BF16), proj_ref[...]) * gate


def _k3_kernel(y_ref, g_ref, bv_ref, yb_ref, x_ref, p_ref, gnmean_ref, lnw_ref, lnb_ref, wout_ref,
               nffn_ref, up_ref, down_ref, nple_ref, gate_ref, proj_ref, nnext_ref, h_ref, hn_ref):
    y = y_ref[...]
    d = y - _dot_wide_lhs(y, gnmean_ref[...], 2)
    var = _dot_wide_lhs(d * d, gnmean_ref[...], 2)
    yn = d * lax.rsqrt(var + GN_EPS) * lnw_ref[...] + lnb_ref[...]
    ya = ((yn + bv_ref[...]) * g_ref[...]).astype(BF16)
    h = x_ref[...] + _dot(ya, wout_ref[0:D_RWKV, :]) + _dot(yb_ref[...], wout_ref[D_RWKV:, :])
    h = _tail(h, p_ref, nffn_ref, up_ref, down_ref, nple_ref, gate_ref, proj_ref)
    h_ref[...] = h
    hn_ref[...] = _rms(h, nnext_ref[...])


def _k3(y, g, bv, yb, x2d, p2d, ln_w, ln_b, wout, nffn, up, down, nple, gate, proj, nnext):
    t = x2d.shape[0]
    rows = TAIL_ROWS
    lane = np.arange(D_RWKV)
    gnmean = jnp.asarray(((lane[:, None] // HEAD) == (lane[None, :] // HEAD)) / HEAD, BF16)
    consts = [gnmean, ln_w, ln_b, wout, nffn, up, down, nple, gate, proj, nnext]
    return pl.pallas_call(
        _k3_kernel,
        grid=(t // rows,),
        in_specs=[_row_spec(rows, D_RWKV), _row_spec(rows, D_RWKV), _row_spec(rows, D_RWKV),
                  _row_spec(rows, D_CONV), _row_spec(rows, D_MODEL), _row_spec(rows, D_PLE)]
                 + [_const_spec(c.shape) for c in consts],
        out_specs=[_row_spec(rows, D_MODEL), _row_spec(rows, D_MODEL)],
        out_shape=[jax.ShapeDtypeStruct((t, D_MODEL), F32), jax.ShapeDtypeStruct((t, D_MODEL), F32)],
        compiler_params=_params("arbitrary"),
        name="k3_mix_out_mlp_ple",
    )(y, g, bv, yb, x2d, p2d, *consts)


def _shift_rows(x, s):
    n = x.shape[0]
    if s % 8 == 0:
        return jnp.concatenate([jnp.zeros((s, x.shape[1]), x.dtype), x[:n - s]], axis=0)
    keep = lax.broadcasted_iota(jnp.int32, x.shape, 0) >= s
    return jnp.where(keep, pltpu.roll(x, s, axis=0), 0.0)


def _k4_kernel(u_ref, toep_ref, pre_ref, pim_ref, qre_ref, qim_ref, are_ref, aim_ref, dskip_ref, y_ref):
    n = u_ref.shape[1]
    u0, u1 = u_ref[0], u_ref[1]
    ub0, ub1 = u0.astype(BF16), u1.astype(BF16)
    ucat = jnp.concatenate([ub0, ub1], axis=1)
    xr = _dot(ucat, pre_ref[0])
    xi = _dot(ucat, pim_ref[0])
    levels = are_ref.shape[1]
    for k in range(levels):
        s = 1 << k
        if s >= n:
            break
        ar, ai = are_ref[0, k:k + 1, :], aim_ref[0, k:k + 1, :]
        sr, si = _shift_rows(xr, s), _shift_rows(xi, s)
        xr, xi = xr + ar * sr - ai * si, xi + ar * si + ai * sr
    hr, hi = _shift_rows(xr, 1), _shift_rows(xi, 1)
    y_in = _dot(hr.astype(BF16), qre_ref[0]) + _dot(hi.astype(BF16), qim_ref[0])
    w = S5_BLOCK * SSM_GROUP
    dsk = dskip_ref[0]
    y0 = _dot(ub0, toep_ref[0]) + y_in[:, :w] + dsk[:, :w] * u0
    y1 = _dot(ub1, toep_ref[1]) + y_in[:, w:] + dsk[:, w:] * u1
    y_ref[0] = _gelu_tanh(y0).astype(BF16)
    y_ref[1] = _gelu_tanh(y1).astype(BF16)


def _k4(ug, toep, pre2, pim2, qre2, qim2, are, aim, dskip2):
    groups, n, w = ug.shape
    pairs = groups // 2
    lead = lambda q: (q, 0, 0)
    return pl.pallas_call(
        _k4_kernel,
        grid=(pairs,),
        in_specs=[pl.BlockSpec((2, n, w), lead), pl.BlockSpec((2, w, w), lead),
                  pl.BlockSpec((1,) + pre2.shape[1:], lead), pl.BlockSpec((1,) + pim2.shape[1:], lead),
                  pl.BlockSpec((1,) + qre2.shape[1:], lead), pl.BlockSpec((1,) + qim2.shape[1:], lead),
                  pl.BlockSpec((1,) + are.shape[1:], lead), pl.BlockSpec((1,) + aim.shape[1:], lead),
                  pl.BlockSpec((1,) + dskip2.shape[1:], lead)],
        out_specs=pl.BlockSpec((2, n, w), lead),
        out_shape=jax.ShapeDtypeStruct((groups, n, w), BF16),
        compiler_params=_params("arbitrary"),
        name="k4_s5",
    )(ug, toep, pre2, pim2, qre2, qim2, are, aim, dskip2)


def _s5_tables(lam_re, lam_im, log_step, b_re, b_im, c_re, c_im, d_skip, n_blocks):
    g, p, c, nb = SSM_GROUPS, SSM_STATE, SSM_GROUP, S5_BLOCK
    lre = jnp.minimum(lam_re.astype(F32), -1e-4)
    lim = lam_im.astype(F32)
    step = jnp.exp(log_step.astype(F32))[:, None]
    ar, ai = lre * step, lim * step
    n = jnp.arange(nb + 1, dtype=F32)[:, None, None]
    mag = jnp.exp(n * ar)
    pr, pi = mag * jnp.cos(n * ai), mag * jnp.sin(n * ai)
    nr, ni = pr[1] - 1.0, pi[1]
    den = lre * lre + lim * lim
    qr, qi = (nr * lre + ni * lim) / den, (ni * lre - nr * lim) / den
    bre, bim = b_re.astype(F32), b_im.astype(F32)
    bbr = qr[..., None] * bre - qi[..., None] * bim
    bbi = qr[..., None] * bim + qi[..., None] * bre
    mr = pr[:nb, :, :, None] * bbr - pi[:nb, :, :, None] * bbi
    mi = pr[:nb, :, :, None] * bbi + pi[:nb, :, :, None] * bbr
    cre, cim = c_re.astype(F32), c_im.astype(F32)
    taps = (jnp.einsum('gop,tgpc->gtoc', cre, mr, precision=HIGHEST)
            - jnp.einsum('gop,tgpc->gtoc', cim, mi, precision=HIGHEST))
    pos = np.arange(nb)
    lag = pos[None, :] - pos[:, None]
    toep = jnp.where((lag >= 0)[None, :, :, None, None], taps[:, np.clip(lag, 0, nb - 1)], 0.0)
    toep = toep.transpose(0, 1, 4, 2, 3).reshape(g, nb * c, nb * c)
    p_re = mr[::-1].transpose(1, 0, 3, 2).reshape(g, nb * c, p)
    p_im = mi[::-1].transpose(1, 0, 3, 2).reshape(g, nb * c, p)
    prt, pit = pr[1:].transpose(1, 0, 2)[:, :, None, :], pi[1:].transpose(1, 0, 2)[:, :, None, :]
    q_re = (cre[:, None] * prt - cim[:, None] * pit).transpose(0, 3, 1, 2).reshape(g, p, nb * c)
    q_im = -(cre[:, None] * pit + cim[:, None] * prt).transpose(0, 3, 1, 2).reshape(g, p, nb * c)
    levels = max(1, int(np.ceil(np.log2(max(n_blocks, 2)))))
    a_re, a_im = [pr[nb]], [pi[nb]]
    for _ in range(levels - 1):
        r, i = a_re[-1], a_im[-1]
        a_re.append(r * r - i * i)
        a_im.append(2.0 * r * i)
    a_re, a_im = jnp.stack(a_re, axis=1), jnp.stack(a_im, axis=1)

    def pair_diag(m):
        r, k = m.shape[1:]
        m = m.reshape(g // 2, 2, r, k)
        return jnp.einsum('qark,ab->qarbk', m, jnp.eye(2, dtype=m.dtype)).reshape(g // 2, 2 * r, 2 * k)

    def pair_lanes(m):
        r, k = m.shape[1:]
        return m.reshape(g // 2, 2, r, k).transpose(0, 2, 1, 3).reshape(g // 2, r, 2 * k)

    dsk = jnp.tile(d_skip.astype(F32).reshape(g, 1, c), (1, nb, 1)).reshape(g, 1, nb * c)
    return (toep.astype(BF16), pair_diag(p_re).astype(BF16), pair_diag(p_im).astype(BF16),
            pair_diag(q_re).astype(BF16), pair_diag(q_im).astype(BF16),
            pair_lanes(a_re), pair_lanes(a_im), pair_lanes(dsk))


def _k5_kernel(yg_ref, h_ref, p_ref, w1_ref, w2_ref, nffn_ref, up_ref, down_ref, nple_ref, gate_ref,
               proj_ref, nfinal_ref, o_ref):
    yg = yg_ref[...]
    h = h_ref[...] + _dot(yg, w1_ref[...]) * _sigmoid(_dot(yg, w2_ref[...]))
    h = _tail(h, p_ref, nffn_ref, up_ref, down_ref, nple_ref, gate_ref, proj_ref)
    o_ref[...] = _rms(h, nfinal_ref[...])


def _k5(yg, h, p2d, w1, w2, nffn, up, down, nple, gate, proj, nfinal):
    t = h.shape[0]
    rows = TAIL_ROWS
    consts = [w1, w2, nffn, up, down, nple, gate, proj, nfinal]
    return pl.pallas_call(
        _k5_kernel,
        grid=(t // rows,),
        in_specs=[_row_spec(rows, D_MODEL), _row_spec(rows, D_MODEL), _row_spec(rows, D_PLE)]
                 + [_const_spec(c.shape) for c in consts],
        out_specs=_row_spec(rows, D_MODEL),
        out_shape=jax.ShapeDtypeStruct((t, D_MODEL), F32),
        compiler_params=_params("arbitrary"),
        name="k5_glu_mlp_ple_norm",
    )(yg, h, p2d, *consts)


def _row(vec):
    return vec.astype(F32).reshape(1, -1)


def _dup_heads(m):
    lead = m.shape[:-1]
    m = m.reshape(lead + (N_HEADS, 1, HEAD))
    return jnp.broadcast_to(m, lead + (N_HEADS, 2, HEAD)).reshape(lead + (2 * D_RWKV,))


def kernel(x, p, l0_norm_mix, l0_w_in, l0_shift_mu, l0_w0, l0_w_lora_up, l0_a0, l0_a_lora_up, l0_g_lora_up, l0_k_k, l0_k_a, l0_r_k, l0_ln_w, l0_ln_b, l0_conv_w, l0_w_out, l0_norm_ffn, l0_ffn_up, l0_ffn_down, l0_norm_ple, l0_ple_gate, l0_ple_proj, l1_norm_mix, l1_lambda_re, l1_lambda_im, l1_log_step, l1_b_re, l1_b_im, l1_c_re, l1_c_im, l1_d_skip, l1_glu_w1, l1_glu_w2, l1_norm_ffn, l1_ffn_up, l1_ffn_down, l1_norm_ple, l1_ple_gate, l1_ple_proj, norm_final):
    bsz, t, _ = x.shape
    assert bsz == 1 and t % K2_ROWS == 0 and t % TAIL_ROWS == 0
    x2d = x.reshape(t, D_MODEL)
    p0, p1 = p[0].reshape(t, D_PLE), p[1].reshape(t, D_PLE)

    cols = np.arange(N_HEADS)[:, None] * HEAD + np.arange(HEAD)[None, :]
    k_cols, r_cols = D_RWKV + cols, cols
    idx = np.concatenate([np.stack([k_cols, k_cols], axis=1).reshape(-1),
                          np.stack([k_cols, r_cols], axis=1).reshape(-1),
                          np.arange(2 * D_RWKV, l0_w_in.shape[1])])
    win2 = l0_w_in[:, idx].astype(BF16)
    mu2 = _row(l0_shift_mu[idx[:W_ZA]])
    ww = jnp.zeros((D_LORA, W_KK), F32).at[:LORA_W].set(_dup_heads(l0_w_lora_up)).astype(BF16)
    wa = jnp.zeros((D_LORA, W_KK), F32).at[LORA_W:LORA_W + LORA_A].set(_dup_heads(l0_a_lora_up)).astype(BF16)
    wg = jnp.zeros((D_LORA, D_RWKV), F32).at[LORA_W + LORA_A:].set(l0_g_lora_up).astype(BF16)
    rk2 = jnp.stack([jnp.zeros((N_HEADS, HEAD), F32), l0_r_k.astype(F32)], axis=1).reshape(1, W_KR)

    ab, kr, v, gam, g, bv, yb = _k1(
        x2d, _row(l0_norm_mix), win2, mu2, _dup_heads(_row(l0_w0)), ww, _dup_heads(_row(l0_a0)), wa, wg,
        _dup_heads(_row(l0_k_k)), _dup_heads(_row(l0_k_a)), rk2, l0_conv_w.astype(F32))
    y = _k2(ab, kr, v, gam)
    h, hn = _k3(y, g, bv, yb, x2d, p0, _row(l0_ln_w), _row(l0_ln_b), l0_w_out.astype(BF16),
                _row(l0_norm_ffn), l0_ffn_up.astype(BF16), l0_ffn_down.astype(BF16),
                _row(l0_norm_ple), l0_ple_gate.astype(BF16), l0_ple_proj.astype(BF16), _row(l1_norm_mix))

    nblk = t // S5_BLOCK
    ug = hn.reshape(nblk, S5_BLOCK, SSM_GROUPS, SSM_GROUP).transpose(2, 0, 1, 3).reshape(
        SSM_GROUPS, nblk, S5_BLOCK * SSM_GROUP)
    tables = _s5_tables(l1_lambda_re, l1_lambda_im, l1_log_step, l1_b_re, l1_b_im, l1_c_re, l1_c_im,
                        l1_d_skip, nblk)
    yg = _k4(ug, *tables)
    yg = yg.reshape(SSM_GROUPS, nblk, S5_BLOCK, SSM_GROUP).transpose(1, 2, 0, 3).reshape(t, D_MODEL)

    out = _k5(yg, h, p1, l1_glu_w1.astype(BF16), l1_glu_w2.astype(BF16), _row(l1_norm_ffn),
              l1_ffn_up.astype(BF16), l1_ffn_down.astype(BF16), _row(l1_norm_ple),
              l1_ple_gate.astype(BF16), l1_ple_proj.astype(BF16), _row(norm_final))
    return out.reshape(bsz, t, D_MODEL)
```

```python
import numpy as np
import jax
import jax.numpy as jnp
from jax import lax
from jax.experimental import pallas as pl
from jax.experimental.pallas import tpu as pltpu

F32 = jnp.float32
BF16 = jnp.bfloat16

D_MODEL = 1024
N_HEADS = 8
HEAD = 64
D_RWKV = N_HEADS * HEAD
D_CONV = D_MODEL - D_RWKV
LORA_W, LORA_A, LORA_G = 64, 64, 128
D_LORA = LORA_W + LORA_A + LORA_G
D_IN_A = 3 * D_RWKV + D_LORA
D_IN = D_IN_A + 3 * D_CONV
GN_EPS = 64e-5
RMS_EPS = 1e-6
SSM_GROUPS, SSM_GROUP, SSM_STATE = 64, 16, 64
D_FF = 4 * D_MODEL
D_PLE = 256

VMEM_LIMIT = 56 * 1024 * 1024
HIGHEST = lax.Precision.HIGHEST

RWKV_CHUNK = 64
K1_ROWS = 256
K2_CHUNKS = 2
TAIL_ROWS = 512
FF_CHUNK = 1024
S5_BLOCK = 16


def _dot(a, b):
    return jnp.dot(a, b, preferred_element_type=F32)


def _dot_nt(a, b):
    return lax.dot_general(a, b, (((1,), (1,)), ((), ())), preferred_element_type=F32)


def _dot_tn(a, b):
    return lax.dot_general(a, b, (((0,), (0,)), ((), ())), preferred_element_type=F32)


def _split_bf16(x, n):
    pieces = []
    for _ in range(n - 1):
        p = x.astype(BF16)
        pieces.append(p)
        x = x - p.astype(F32)
    pieces.append(x.astype(BF16))
    return pieces


def _dot_wide_lhs(x, w, n):
    return sum(_dot(p, w) for p in _split_bf16(x, n))


def _dot_wide_rhs(w, x, n):
    return sum(_dot(w, p) for p in _split_bf16(x, n))


def _rms(x, gain):
    return x * lax.rsqrt(jnp.mean(x * x, axis=-1, keepdims=True) + RMS_EPS) * gain


def _sigmoid(x):
    return 1.0 / (1.0 + jnp.exp(-x))


def _softplus(x):
    return jnp.maximum(x, 0.0) + jnp.log(1.0 + jnp.exp(-jnp.abs(x)))


def _gelu_tanh(x):
    return 0.5 * x * (1.0 + jnp.tanh(np.sqrt(2.0 / np.pi).astype(np.float32) * (x + 0.044715 * (x * x * x))))


def _const_spec(shape):
    nd = len(shape)
    return pl.BlockSpec(shape, lambda *_: (0,) * nd, pipeline_mode=pl.Buffered(1))


def _row_spec(rows, cols):
    return pl.BlockSpec((rows, cols), lambda i: (i, 0))


def _params(*sem):
    return pltpu.CompilerParams(dimension_semantics=sem, vmem_limit_bytes=VMEM_LIMIT)


def _head_ones(scale):
    lane = np.arange(D_RWKV)
    return jnp.asarray(scale * ((lane[:, None] // HEAD) == (lane[None, :] // HEAD)), BF16)


def _k1_kernel(x_ref, gmix_ref, win_ref, mu_ref, w0_ref, ww_ref, a0_ref, wa_ref, wg_ref, kk_ref, ka_ref,
               rk_ref, cw_ref, tri_ref, blk_ref, sel_ref, hsum_ref,
               at_ref, bt_ref, kt_ref, rt_ref, v_ref, bg_ref, kg_ref, gam_ref, g_ref, bv_ref, yb_ref,
               zs_ref, us_ref):
    rows = x_ref.shape[0]

    @pl.when(pl.program_id(0) == 0)
    def _():
        zs_ref[0:8, :] = jnp.zeros((8, D_IN_A), F32)
        us_ref[0:8, :] = jnp.zeros((8, D_CONV), F32)

    xn = _rms(x_ref[...], gmix_ref[...])
    z = _dot(xn.astype(BF16), win_ref[...])

    za = z[:, :D_IN_A]
    zs_ref[8:8 + rows, :] = za
    za_prev = zs_ref[7:7 + rows, :]
    zs_ref[7:8, :] = za[rows - 1:rows, :]
    za = za + mu_ref[...] * (za_prev - za)

    r = za[:, 0:D_RWKV]
    k = za[:, D_RWKV:2 * D_RWKV]
    v = za[:, 2 * D_RWKV:3 * D_RWKV]
    lora_in = za[:, 3 * D_RWKV:]

    w_log = -_softplus(-(w0_ref[...] + _dot(jnp.tanh(lora_in).astype(BF16), ww_ref[...]))) - 0.5
    logw = -jnp.exp(w_log)
    lr = _sigmoid(a0_ref[...] + _dot(lora_in.astype(BF16), wa_ref[...]))
    g_ref[...] = _dot(_sigmoid(lora_in).astype(BF16), wg_ref[...])

    lw = _split_bf16(logw, 3)
    c_incl = sum(_dot(tri_ref[...], p) for p in lw)
    c_tot = sum(_dot(blk_ref[...], p) for p in lw)
    gam_ref[...] = jnp.exp(sum(_dot(sel_ref[...], p) for p in lw))
    e_neg = jnp.exp(-c_incl)
    e_rest = jnp.exp(c_tot - c_incl)

    kk = k * kk_ref[...]
    kk = kk / jnp.maximum(jnp.sqrt(_dot_wide_lhs(kk * kk, hsum_ref[...], 2)), 1e-12)
    kmod = k * (1.0 + (lr - 1.0) * ka_ref[...])
    at_ref[...] = (-kk * jnp.exp(c_incl - logw)).astype(BF16)
    bt_ref[...] = (kk * lr * e_neg).astype(BF16)
    bg_ref[...] = (kk * lr * e_rest).astype(BF16)
    kt_ref[...] = (kmod * e_neg).astype(BF16)
    kg_ref[...] = (kmod * e_rest).astype(BF16)
    rt_ref[...] = (r * jnp.exp(c_incl)).astype(BF16)
    v_ref[...] = v.astype(BF16)
    bv_ref[...] = _dot_wide_lhs(r * kmod * rk_ref[...], hsum_ref[...], 2) * v

    zb = z[:, D_IN_A:]
    b_gate = zb[:, :D_CONV]
    u = zb[:, D_CONV:2 * D_CONV] * zb[:, 2 * D_CONV:]
    us_ref[8:8 + rows, :] = u
    u1 = us_ref[7:7 + rows, :]
    u2 = us_ref[6:6 + rows, :]
    us_ref[6:8, :] = u[rows - 2:rows, :]
    cw = cw_ref[...]
    yb_ref[...] = (b_gate * (cw[0:1] * u + cw[1:2] * u1 + cw[2:3] * u2)).astype(BF16)


def _k1(x2d, gmix, win, mu, w0, ww, a0, wa, wg, k_k, k_a, r_k, conv_w):
    t = x2d.shape[0]
    rows = K1_ROWS
    nchunk = rows // RWKV_CHUNK
    r_idx = np.arange(rows)
    same = (r_idx[:, None] // RWKV_CHUNK) == (r_idx[None, :] // RWKV_CHUNK)
    tri = jnp.asarray(same & (r_idx[None, :] <= r_idx[:, None]), BF16)
    blk = jnp.asarray(same, BF16)
    sel = jnp.asarray(np.arange(8)[:, None] == (r_idx[None, :] // RWKV_CHUNK), BF16)

    consts = [gmix, win, mu, w0, ww, a0, wa, wg, k_k, k_a, r_k, conv_w, tri, blk, sel, _head_ones(1.0)]
    wide = jax.ShapeDtypeStruct((t, D_RWKV), BF16)
    out_shape = [wide] * 7 + [
        jax.ShapeDtypeStruct((t // rows, 8, D_RWKV), F32),
        jax.ShapeDtypeStruct((t, D_RWKV), F32),
        jax.ShapeDtypeStruct((t, D_RWKV), F32),
        jax.ShapeDtypeStruct((t, D_CONV), BF16),
    ]
    out_specs = [_row_spec(rows, D_RWKV)] * 7 + [
        pl.BlockSpec((None, 8, D_RWKV), lambda i: (i, 0, 0)),
        _row_spec(rows, D_RWKV), _row_spec(rows, D_RWKV), _row_spec(rows, D_CONV)]
    return pl.pallas_call(
        _k1_kernel,
        grid=(t // rows,),
        in_specs=[_row_spec(rows, D_MODEL)] + [_const_spec(c.shape) for c in consts],
        out_specs=out_specs,
        out_shape=out_shape,
        scratch_shapes=[pltpu.VMEM((rows + 8, D_IN_A), F32), pltpu.VMEM((rows + 8, D_CONV), F32)],
        compiler_params=_params("arbitrary"),
        name="k1_inproj_prep",
    )(x2d, *consts)


def _k2_kernel(at_ref, bt_ref, kt_ref, rt_ref, v_ref, bg_ref, kg_ref, gam_ref, y_ref, s_ref):
    n = RWKV_CHUNK
    step = pl.program_id(0)

    @pl.when(step == 0)
    def _():
        s_ref[...] = jnp.zeros_like(s_ref)

    row = lax.broadcasted_iota(jnp.int32, (n, n), 0)
    col = lax.broadcasted_iota(jnp.int32, (n, n), 1)
    strict, incl, diag = col < row, col <= row, col == row
    eye = jnp.where(diag, 1.0, 0.0).astype(F32)
    sub = lax.broadcasted_iota(jnp.int32, (8, D_RWKV), 0)
    first = (step * K2_CHUNKS) % (K1_ROWS // n)

    units = [(c, h) for c in range(K2_CHUNKS) for h in range(N_HEADS)]

    def tile(ref, c, h):
        return ref[c * n:(c + 1) * n, h * HEAD:(h + 1) * HEAD]

    a = {u: tile(at_ref, *u) for u in units}
    r = {u: tile(rt_ref, *u) for u in units}
    v = {u: tile(v_ref, *u) for u in units}
    gb, gk = {}, {}
    for u in units:
        ar = jnp.concatenate([a[u], r[u]], axis=0)
        gb[u], gk[u] = _dot_nt(ar, tile(bt_ref, *u)), _dot_nt(ar, tile(kt_ref, *u))
    q = {u: jnp.where(strict, gb[u][:n], 0.0) for u in units}
    a_rb = {u: jnp.where(incl, gb[u][n:], 0.0).astype(BF16) for u in units}
    wv = {u: _dot(jnp.concatenate([jnp.where(strict, gk[u][:n], 0.0), jnp.where(incl, gk[u][n:], 0.0)],
                                  axis=0).astype(BF16), v[u]) for u in units}
    tinv = {u: eye + q[u] for u in units}
    for _ in range(5):
        for u in units:
            qb = q[u].astype(BF16)
            q[u] = _dot(qb, qb)
        for u in units:
            tinv[u] = tinv[u] + _dot(tinv[u].astype(BF16), q[u].astype(BF16))
    a_hat, u_loc = {}, {}
    for u in units:
        tb = tinv[u].astype(BF16)
        a_hat[u] = _dot(tb, a[u]).astype(BF16)
        u_loc[u] = _dot(tb, wv[u][:n].astype(BF16)).astype(BF16)
    r_hat = {u: r[u].astype(F32) + _dot(a_rb[u], a_hat[u]) for u in units}
    y_loc = {u: wv[u][n:] + _dot(a_rb[u], u_loc[u]) for u in units}
    m_t, c_t = {}, {}
    for u in units:
        c, h = u
        gam_row = jnp.sum(jnp.where(sub == first + c, gam_ref[...], 0.0), axis=0, keepdims=True)
        bg, kg = tile(bg_ref, *u), tile(kg_ref, *u)
        m_t[u] = jnp.where(diag, gam_row[:, h * HEAD:(h + 1) * HEAD], 0.0) + _dot_tn(bg, a_hat[u])
        c_t[u] = _dot_tn(bg, u_loc[u]) + _dot_tn(kg, v[u])
    for u in units:
        c, h = u
        mr = _dot(jnp.concatenate([m_t[u], r_hat[u]], axis=0).astype(BF16), s_ref[h].astype(BF16))
        s_ref[h] = mr[:HEAD] + c_t[u]
        y_ref[c * n:(c + 1) * n, h * HEAD:(h + 1) * HEAD] = mr[HEAD:] + y_loc[u]


def _k2(at, bt, kt, rt, v, bg, kg, gam):
    t = at.shape[0]
    rows = RWKV_CHUNK * K2_CHUNKS
    per = K1_ROWS // rows
    seq = _row_spec(rows, D_RWKV)
    return pl.pallas_call(
        _k2_kernel,
        grid=(t // rows,),
        in_specs=[seq] * 7 + [pl.BlockSpec((None, 8, D_RWKV), lambda i: (i // per, 0, 0))],
        out_specs=seq,
        out_shape=jax.ShapeDtypeStruct((t, D_RWKV), F32),
        scratch_shapes=[pltpu.VMEM((N_HEADS, HEAD, HEAD), F32)],
        compiler_params=_params("arbitrary"),
        name="k2_rwkv_chunked",
    )(at, bt, kt, rt, v, bg, kg, gam)


def _tail(h, p_ref, nffn_ref, up_ref, down_ref, nple_ref, gate_ref, proj_ref):
    hn = _rms(h, nffn_ref[...]).astype(BF16)
    mlp = None
    for c in range(D_FF // FF_CHUNK):
        a = jnp.maximum(_dot(hn, up_ref[:, c * FF_CHUNK:(c + 1) * FF_CHUNK]), 0.0)
        part = _dot((a * a).astype(BF16), down_ref[c * FF_CHUNK:(c + 1) * FF_CHUNK, :])
        mlp = part if mlp is None else mlp + part
    h = h + mlp
    gate = _sigmoid(_dot(_rms(h, nple_ref[...]).astype(BF16), gate_ref[...]))
    return h + _dot(p_ref[...].astype(BF16), proj_ref[...]) * gate


def _k3_kernel(y_ref, g_ref, bv_ref, yb_ref, x_ref, p_ref, gnmean_ref, lnw_ref, lnb_ref, wout_ref,
               nffn_ref, up_ref, down_ref, nple_ref, gate_ref, proj_ref, nnext_ref, h_ref, hn_ref):
    y = y_ref[...]
    d = y - _dot_wide_lhs(y, gnmean_ref[...], 2)
    var = _dot_wide_lhs(d * d, gnmean_ref[...], 2)
    yn = d * lax.rsqrt(var + GN_EPS) * lnw_ref[...] + lnb_ref[...]
    ya = ((yn + bv_ref[...]) * g_ref[...]).astype(BF16)
    h = x_ref[...] + _dot(ya, wout_ref[0:D_RWKV, :]) + _dot(yb_ref[...], wout_ref[D_RWKV:, :])
    h = _tail(h, p_ref, nffn_ref, up_ref, down_ref, nple_ref, gate_ref, proj_ref)
    h_ref[...] = h
    hn_ref[...] = _rms(h, nnext_ref[...])


def _k3(y, g, bv, yb, x2d, p2d, ln_w, ln_b, wout, nffn, up, down, nple, gate, proj, nnext):
    t = x2d.shape[0]
    rows = TAIL_ROWS
    consts = [_head_ones(1.0 / HEAD), ln_w, ln_b, wout, nffn, up, down, nple, gate, proj, nnext]
    return pl.pallas_call(
        _k3_kernel,
        grid=(t // rows,),
        in_specs=[_row_spec(rows, D_RWKV), _row_spec(rows, D_RWKV), _row_spec(rows, D_RWKV),
                  _row_spec(rows, D_CONV), _row_spec(rows, D_MODEL), _row_spec(rows, D_PLE)]
                 + [_const_spec(c.shape) for c in consts],
        out_specs=[_row_spec(rows, D_MODEL), _row_spec(rows, D_MODEL)],
        out_shape=[jax.ShapeDtypeStruct((t, D_MODEL), F32), jax.ShapeDtypeStruct((t, D_MODEL), F32)],
        compiler_params=_params("arbitrary"),
        name="k3_mix_out_mlp_ple",
    )(y, g, bv, yb, x2d, p2d, *consts)


def _shift_rows(x, s):
    n = x.shape[0]
    if s % 8 == 0:
        return jnp.concatenate([jnp.zeros((s, x.shape[1]), x.dtype), x[:n - s]], axis=0)
    keep = lax.broadcasted_iota(jnp.int32, x.shape, 0) >= s
    return jnp.where(keep, pltpu.roll(x, s, axis=0), 0.0)


def _k4_kernel(u_ref, toep_ref, pre_ref, pim_ref, qre_ref, qim_ref, are_ref, aim_ref, dskip_ref, y_ref):
    n = u_ref.shape[1]
    u0, u1 = u_ref[0], u_ref[1]
    ub0, ub1 = u0.astype(BF16), u1.astype(BF16)
    ucat = jnp.concatenate([ub0, ub1], axis=1)
    xr = _dot(ucat, pre_ref[0])
    xi = _dot(ucat, pim_ref[0])
    levels = are_ref.shape[1]
    for k in range(levels):
        s = 1 << k
        if s >= n:
            break
        ar, ai = are_ref[0, k:k + 1, :], aim_ref[0, k:k + 1, :]
        sr, si = _shift_rows(xr, s), _shift_rows(xi, s)
        xr, xi = xr + ar * sr - ai * si, xi + ar * si + ai * sr
    hr, hi = _shift_rows(xr, 1), _shift_rows(xi, 1)
    y_in = _dot(hr.astype(BF16), qre_ref[0]) + _dot(hi.astype(BF16), qim_ref[0])
    w = S5_BLOCK * SSM_GROUP
    dsk = dskip_ref[0]
    y0 = _dot(ub0, toep_ref[0]) + y_in[:, :w] + dsk[:, :w] * u0
    y1 = _dot(ub1, toep_ref[1]) + y_in[:, w:] + dsk[:, w:] * u1
    y_ref[0] = _gelu_tanh(y0).astype(BF16)
    y_ref[1] = _gelu_tanh(y1).astype(BF16)


def _k4(ug, toep, pre2, pim2, qre2, qim2, are, aim, dskip2):
    groups, n, w = ug.shape
    pairs = groups // 2
    lead = lambda q: (q, 0, 0)
    return pl.pallas_call(
        _k4_kernel,
        grid=(pairs,),
        in_specs=[pl.BlockSpec((2, n, w), lead), pl.BlockSpec((2, w, w), lead),
                  pl.BlockSpec((1,) + pre2.shape[1:], lead), pl.BlockSpec((1,) + pim2.shape[1:], lead),
                  pl.BlockSpec((1,) + qre2.shape[1:], lead), pl.BlockSpec((1,) + qim2.shape[1:], lead),
                  pl.BlockSpec((1,) + are.shape[1:], lead), pl.BlockSpec((1,) + aim.shape[1:], lead),
                  pl.BlockSpec((1,) + dskip2.shape[1:], lead)],
        out_specs=pl.BlockSpec((2, n, w), lead),
        out_shape=jax.ShapeDtypeStruct((groups, n, w), BF16),
        compiler_params=_params("arbitrary"),
        name="k4_s5",
    )(ug, toep, pre2, pim2, qre2, qim2, are, aim, dskip2)


def _s5_tables(lam_re, lam_im, log_step, b_re, b_im, c_re, c_im, d_skip, n_blocks):
    g, p, c, nb = SSM_GROUPS, SSM_STATE, SSM_GROUP, S5_BLOCK
    lre = jnp.minimum(lam_re.astype(F32), -1e-4)
    lim = lam_im.astype(F32)
    step = jnp.exp(log_step.astype(F32))[:, None]
    ar, ai = lre * step, lim * step
    n = jnp.arange(nb + 1, dtype=F32)[:, None, None]
    mag = jnp.exp(n * ar)
    pr, pi = mag * jnp.cos(n * ai), mag * jnp.sin(n * ai)
    nr, ni = pr[1] - 1.0, pi[1]
    den = lre * lre + lim * lim
    qr, qi = (nr * lre + ni * lim) / den, (ni * lre - nr * lim) / den
    bre, bim = b_re.astype(F32), b_im.astype(F32)
    bbr = qr[..., None] * bre - qi[..., None] * bim
    bbi = qr[..., None] * bim + qi[..., None] * bre
    mr = pr[:nb, :, :, None] * bbr - pi[:nb, :, :, None] * bbi
    mi = pr[:nb, :, :, None] * bbi + pi[:nb, :, :, None] * bbr
    cre, cim = c_re.astype(F32), c_im.astype(F32)
    taps = (jnp.einsum('gop,tgpc->gtoc', cre, mr, precision=HIGHEST)
            - jnp.einsum('gop,tgpc->gtoc', cim, mi, precision=HIGHEST))
    pos = np.arange(nb)
    lag = pos[None, :] - pos[:, None]
    toep = jnp.where((lag >= 0)[None, :, :, None, None], taps[:, np.clip(lag, 0, nb - 1)], 0.0)
    toep = toep.transpose(0, 1, 4, 2, 3).reshape(g, nb * c, nb * c)
    p_re = mr[::-1].transpose(1, 0, 3, 2).reshape(g, nb * c, p)
    p_im = mi[::-1].transpose(1, 0, 3, 2).reshape(g, nb * c, p)
    prt, pit = pr[1:].transpose(1, 0, 2)[:, :, None, :], pi[1:].transpose(1, 0, 2)[:, :, None, :]
    q_re = (cre[:, None] * prt - cim[:, None] * pit).transpose(0, 3, 1, 2).reshape(g, p, nb * c)
    q_im = -(cre[:, None] * pit + cim[:, None] * prt).transpose(0, 3, 1, 2).reshape(g, p, nb * c)
    levels = max(1, int(np.ceil(np.log2(max(n_blocks, 2)))))
    a_re, a_im = [pr[nb]], [pi[nb]]
    for _ in range(levels - 1):
        r, i = a_re[-1], a_im[-1]
        a_re.append(r * r - i * i)
        a_im.append(2.0 * r * i)
    a_re, a_im = jnp.stack(a_re, axis=1), jnp.stack(a_im, axis=1)

    def pair_diag(m):
        r, k = m.shape[1:]
        m = m.reshape(g // 2, 2, r, k)
        return jnp.einsum('qark,ab->qarbk', m, jnp.eye(2, dtype=m.dtype)).reshape(g // 2, 2 * r, 2 * k)

    def pair_lanes(m):
        r, k = m.shape[1:]
        return m.reshape(g // 2, 2, r, k).transpose(0, 2, 1, 3).reshape(g // 2, r, 2 * k)

    dsk = jnp.tile(d_skip.astype(F32).reshape(g, 1, c), (1, nb, 1)).reshape(g, 1, nb * c)
    return (toep.astype(BF16), pair_diag(p_re).astype(BF16), pair_diag(p_im).astype(BF16),
            pair_diag(q_re).astype(BF16), pair_diag(q_im).astype(BF16),
            pair_lanes(a_re), pair_lanes(a_im), pair_lanes(dsk))


def _k5_kernel(yg_ref, h_ref, p_ref, w1_ref, w2_ref, nffn_ref, up_ref, down_ref, nple_ref, gate_ref,
               proj_ref, nfinal_ref, o_ref):
    yg = yg_ref[...]
    h = h_ref[...] + _dot(yg, w1_ref[...]) * _sigmoid(_dot(yg, w2_ref[...]))
    h = _tail(h, p_ref, nffn_ref, up_ref, down_ref, nple_ref, gate_ref, proj_ref)
    o_ref[...] = _rms(h, nfinal_ref[...])


def _k5(yg, h, p2d, w1, w2, nffn, up, down, nple, gate, proj, nfinal):
    t = h.shape[0]
    rows = TAIL_ROWS
    consts = [w1, w2, nffn, up, down, nple, gate, proj, nfinal]
    return pl.pallas_call(
        _k5_kernel,
        grid=(t // rows,),
        in_specs=[_row_spec(rows, D_MODEL), _row_spec(rows, D_MODEL), _row_spec(rows, D_PLE)]
                 + [_const_spec(c.shape) for c in consts],
        out_specs=_row_spec(rows, D_MODEL),
        out_shape=jax.ShapeDtypeStruct((t, D_MODEL), F32),
        compiler_params=_params("arbitrary"),
        name="k5_glu_mlp_ple_norm",
    )(yg, h, p2d, *consts)


def _row(vec):
    return vec.astype(F32).reshape(1, -1)


def _lora_rows(m, first, width):
    return jnp.zeros((D_LORA, width), F32).at[first:first + m.shape[0]].set(m.astype(F32)).astype(BF16)


def kernel(x, p, l0_norm_mix, l0_w_in, l0_shift_mu, l0_w0, l0_w_lora_up, l0_a0, l0_a_lora_up, l0_g_lora_up, l0_k_k, l0_k_a, l0_r_k, l0_ln_w, l0_ln_b, l0_conv_w, l0_w_out, l0_norm_ffn, l0_ffn_up, l0_ffn_down, l0_norm_ple, l0_ple_gate, l0_ple_proj, l1_norm_mix, l1_lambda_re, l1_lambda_im, l1_log_step, l1_b_re, l1_b_im, l1_c_re, l1_c_im, l1_d_skip, l1_glu_w1, l1_glu_w2, l1_norm_ffn, l1_ffn_up, l1_ffn_down, l1_norm_ple, l1_ple_gate, l1_ple_proj, norm_final):
    bsz, t, _ = x.shape
    assert bsz == 1 and t % K1_ROWS == 0 and t % TAIL_ROWS == 0
    x2d = x.reshape(t, D_MODEL)
    p0, p1 = p[0].reshape(t, D_PLE), p[1].reshape(t, D_PLE)

    at, bt, kt, rt, v, bg, kg, gam, g, bv, yb = _k1(
        x2d, _row(l0_norm_mix), l0_w_in.astype(BF16), _row(l0_shift_mu[:D_IN_A]), _row(l0_w0),
        _lora_rows(l0_w_lora_up, 0, D_RWKV), _row(l0_a0), _lora_rows(l0_a_lora_up, LORA_W, D_RWKV),
        _lora_rows(l0_g_lora_up, LORA_W + LORA_A, D_RWKV), _row(l0_k_k), _row(l0_k_a), _row(l0_r_k),
        l0_conv_w.astype(F32))
    y = _k2(at, bt, kt, rt, v, bg, kg, gam)
    h, hn = _k3(y, g, bv, yb, x2d, p0, _row(l0_ln_w), _row(l0_ln_b), l0_w_out.astype(BF16),
                _row(l0_norm_ffn), l0_ffn_up.astype(BF16), l0_ffn_down.astype(BF16),
                _row(l0_norm_ple), l0_ple_gate.astype(BF16), l0_ple_proj.astype(BF16), _row(l1_norm_mix))

    nblk = t // S5_BLOCK
    ug = hn.reshape(nblk, S5_BLOCK, SSM_GROUPS, SSM_GROUP).transpose(2, 0, 1, 3).reshape(
        SSM_GROUPS, nblk, S5_BLOCK * SSM_GROUP)
    tables = _s5_tables(l1_lambda_re, l1_lambda_im, l1_log_step, l1_b_re, l1_b_im, l1_c_re, l1_c_im,
                        l1_d_skip, nblk)
    yg = _k4(ug, *tables)
    yg = yg.reshape(SSM_GROUPS, nblk, S5_BLOCK, SSM_GROUP).transpose(1, 2, 0, 3).reshape(t, D_MODEL)

    out = _k5(yg, h, p1, l1_glu_w1.astype(BF16), l1_glu_w2.astype(BF16), _row(l1_norm_ffn),
              l1_ffn_up.astype(BF16), l1_ffn_down.astype(BF16), _row(l1_norm_ple),
              l1_ple_gate.astype(BF16), l1_ple_proj.astype(BF16), _row(norm_final))
    return out.reshape(bsz, t, D_MODEL)
```

```python
import numpy as np
import jax
import jax.numpy as jnp
from jax import lax
from jax.experimental import pallas as pl
from jax.experimental.pallas import tpu as pltpu

F32 = jnp.float32
BF16 = jnp.bfloat16

D_MODEL = 1024
N_HEADS = 8
HEAD = 64
D_RWKV = N_HEADS * HEAD
D_CONV = D_MODEL - D_RWKV
LORA_W, LORA_A, LORA_G = 64, 64, 128
D_LORA = LORA_W + LORA_A + LORA_G
D_IN_A = 3 * D_RWKV + D_LORA
D_IN = D_IN_A + 3 * D_CONV
GN_EPS = 64e-5
RMS_EPS = 1e-6
SSM_GROUPS, SSM_GROUP, SSM_STATE = 64, 16, 64
D_FF = 4 * D_MODEL
D_PLE = 256

VMEM_LIMIT = 56 * 1024 * 1024
HIGHEST = lax.Precision.HIGHEST

RWKV_CHUNK = 64
K1_ROWS = 256
K2_CHUNKS = 2
TAIL_ROWS = 512
FF_CHUNK = 1024
S5_BLOCK = 16
S5_ROWS = 512


def _dot(a, b):
    return jnp.dot(a, b, preferred_element_type=F32)


def _dot_nt(a, b):
    return lax.dot_general(a, b, (((1,), (1,)), ((), ())), preferred_element_type=F32)


def _dot_tn(a, b):
    return lax.dot_general(a, b, (((0,), (0,)), ((), ())), preferred_element_type=F32)


def _split_bf16(x, n):
    pieces = []
    for _ in range(n - 1):
        p = x.astype(BF16)
        pieces.append(p)
        x = x - p.astype(F32)
    pieces.append(x.astype(BF16))
    return pieces


def _dot_wide_lhs(x, w, n):
    return sum(_dot(p, w) for p in _split_bf16(x, n))


def _dot_wide_rhs(w, x, n):
    return sum(_dot(w, p) for p in _split_bf16(x, n))


def _rms(x, gain):
    return x * lax.rsqrt(jnp.mean(x * x, axis=-1, keepdims=True) + RMS_EPS) * gain


def _sigmoid(x):
    return 1.0 / (1.0 + jnp.exp(-x))


def _softplus(x):
    return jnp.maximum(x, 0.0) + jnp.log(1.0 + jnp.exp(-jnp.abs(x)))


def _gelu_tanh(x):
    return 0.5 * x * (1.0 + jnp.tanh(np.sqrt(2.0 / np.pi).astype(np.float32) * (x + 0.044715 * (x * x * x))))


def _const_spec(shape):
    nd = len(shape)
    return pl.BlockSpec(shape, lambda *_: (0,) * nd, pipeline_mode=pl.Buffered(1))


def _row_spec(rows, cols):
    return pl.BlockSpec((rows, cols), lambda i: (i, 0))


def _params(*sem):
    return pltpu.CompilerParams(dimension_semantics=sem, vmem_limit_bytes=VMEM_LIMIT)


def _head_ones(scale):
    lane = np.arange(D_RWKV)
    return jnp.asarray(scale * ((lane[:, None] // HEAD) == (lane[None, :] // HEAD)), BF16)


def _k1_kernel(x_ref, gmix_ref, win_ref, mu_ref, w0_ref, ww_ref, a0_ref, wa_ref, wg_ref, kk_ref, ka_ref,
               rk_ref, cw_ref, tri_ref, blk_ref, sel_ref, hsum_ref,
               at_ref, bt_ref, kt_ref, rt_ref, v_ref, bg_ref, kg_ref, gam_ref, g_ref, bv_ref, yb_ref,
               zs_ref, us_ref):
    rows = x_ref.shape[0]

    @pl.when(pl.program_id(0) == 0)
    def _():
        zs_ref[0:8, :] = jnp.zeros((8, D_IN_A), F32)
        us_ref[0:8, :] = jnp.zeros((8, D_CONV), F32)

    xn = _rms(x_ref[...], gmix_ref[...])
    z = _dot(xn.astype(BF16), win_ref[...])

    za = z[:, :D_IN_A]
    zs_ref[8:8 + rows, :] = za
    za_prev = zs_ref[7:7 + rows, :]
    zs_ref[7:8, :] = za[rows - 1:rows, :]
    za = za + mu_ref[...] * (za_prev - za)

    r = za[:, 0:D_RWKV]
    k = za[:, D_RWKV:2 * D_RWKV]
    v = za[:, 2 * D_RWKV:3 * D_RWKV]
    lora_in = za[:, 3 * D_RWKV:]

    w_log = -_softplus(-(w0_ref[...] + _dot(jnp.tanh(lora_in).astype(BF16), ww_ref[...]))) - 0.5
    logw = -jnp.exp(w_log)
    lr = _sigmoid(a0_ref[...] + _dot(lora_in.astype(BF16), wa_ref[...]))
    g_ref[...] = _dot(_sigmoid(lora_in).astype(BF16), wg_ref[...])

    lw = _split_bf16(logw, 3)
    c_incl = sum(_dot(tri_ref[...], p) for p in lw)
    c_tot = sum(_dot(blk_ref[...], p) for p in lw)
    gam_ref[...] = jnp.exp(sum(_dot(sel_ref[...], p) for p in lw))
    e_neg = jnp.exp(-c_incl)
    e_rest = jnp.exp(c_tot - c_incl)

    kk = k * kk_ref[...]
    kk = kk / jnp.maximum(jnp.sqrt(_dot_wide_lhs(kk * kk, hsum_ref[...], 2)), 1e-12)
    kmod = k * (1.0 + (lr - 1.0) * ka_ref[...])
    at_ref[...] = (-kk * jnp.exp(c_incl - logw)).astype(BF16)
    bt_ref[...] = (kk * lr * e_neg).astype(BF16)
    bg_ref[...] = (kk * lr * e_rest).astype(BF16)
    kt_ref[...] = (kmod * e_neg).astype(BF16)
    kg_ref[...] = (kmod * e_rest).astype(BF16)
    rt_ref[...] = (r * jnp.exp(c_incl)).astype(BF16)
    v_ref[...] = v.astype(BF16)
    bv_ref[...] = _dot_wide_lhs(r * kmod * rk_ref[...], hsum_ref[...], 2) * v

    zb = z[:, D_IN_A:]
    b_gate = zb[:, :D_CONV]
    u = zb[:, D_CONV:2 * D_CONV] * zb[:, 2 * D_CONV:]
    us_ref[8:8 + rows, :] = u
    u1 = us_ref[7:7 + rows, :]
    u2 = us_ref[6:6 + rows, :]
    us_ref[6:8, :] = u[rows - 2:rows, :]
    cw = cw_ref[...]
    yb_ref[...] = (b_gate * (cw[0:1] * u + cw[1:2] * u1 + cw[2:3] * u2)).astype(BF16)


def _k1(x2d, gmix, win, mu, w0, ww, a0, wa, wg, k_k, k_a, r_k, conv_w):
    t = x2d.shape[0]
    rows = K1_ROWS
    nchunk = rows // RWKV_CHUNK
    r_idx = np.arange(rows)
    same = (r_idx[:, None] // RWKV_CHUNK) == (r_idx[None, :] // RWKV_CHUNK)
    tri = jnp.asarray(same & (r_idx[None, :] <= r_idx[:, None]), BF16)
    blk = jnp.asarray(same, BF16)
    sel = jnp.asarray(np.arange(8)[:, None] == (r_idx[None, :] // RWKV_CHUNK), BF16)

    consts = [gmix, win, mu, w0, ww, a0, wa, wg, k_k, k_a, r_k, conv_w, tri, blk, sel, _head_ones(1.0)]
    wide = jax.ShapeDtypeStruct((t, D_RWKV), BF16)
    out_shape = [wide] * 7 + [
        jax.ShapeDtypeStruct((t // rows, 8, D_RWKV), F32),
        jax.ShapeDtypeStruct((t, D_RWKV), F32),
        jax.ShapeDtypeStruct((t, D_RWKV), F32),
        jax.ShapeDtypeStruct((t, D_CONV), BF16),
    ]
    out_specs = [_row_spec(rows, D_RWKV)] * 7 + [
        pl.BlockSpec((None, 8, D_RWKV), lambda i: (i, 0, 0)),
        _row_spec(rows, D_RWKV), _row_spec(rows, D_RWKV), _row_spec(rows, D_CONV)]
    return pl.pallas_call(
        _k1_kernel,
        grid=(t // rows,),
        in_specs=[_row_spec(rows, D_MODEL)] + [_const_spec(c.shape) for c in consts],
        out_specs=out_specs,
        out_shape=out_shape,
        scratch_shapes=[pltpu.VMEM((rows + 8, D_IN_A), F32), pltpu.VMEM((rows + 8, D_CONV), F32)],
        compiler_params=_params("arbitrary"),
        name="k1_inproj_prep",
    )(x2d, *consts)


def _k2_kernel(at_ref, bt_ref, kt_ref, rt_ref, v_ref, bg_ref, kg_ref, gam_ref, y_ref, s_ref):
    n = RWKV_CHUNK
    step = pl.program_id(0)

    @pl.when(step == 0)
    def _():
        s_ref[...] = jnp.zeros_like(s_ref)

    row = lax.broadcasted_iota(jnp.int32, (n, n), 0)
    col = lax.broadcasted_iota(jnp.int32, (n, n), 1)
    strict, incl, diag = col < row, col <= row, col == row
    eye = jnp.where(diag, 1.0, 0.0).astype(F32)
    sub = lax.broadcasted_iota(jnp.int32, (8, D_RWKV), 0)
    first = (step * K2_CHUNKS) % (K1_ROWS // n)

    units = [(c, h) for c in range(K2_CHUNKS) for h in range(N_HEADS)]

    def tile(ref, c, h):
        return ref[c * n:(c + 1) * n, h * HEAD:(h + 1) * HEAD]

    a = {u: tile(at_ref, *u) for u in units}
    r = {u: tile(rt_ref, *u) for u in units}
    v = {u: tile(v_ref, *u) for u in units}
    gb, gk = {}, {}
    for u in units:
        ar = jnp.concatenate([a[u], r[u]], axis=0)
        gb[u], gk[u] = _dot_nt(ar, tile(bt_ref, *u)), _dot_nt(ar, tile(kt_ref, *u))
    q = {u: jnp.where(strict, gb[u][:n], 0.0) for u in units}
    a_rb = {u: jnp.where(incl, gb[u][n:], 0.0).astype(BF16) for u in units}
    wv = {u: _dot(jnp.concatenate([jnp.where(strict, gk[u][:n], 0.0), jnp.where(incl, gk[u][n:], 0.0)],
                                  axis=0).astype(BF16), v[u]) for u in units}
    tinv = {u: eye + q[u] for u in units}
    for _ in range(5):
        for u in units:
            qb = q[u].astype(BF16)
            q[u] = _dot(qb, qb)
        for u in units:
            tinv[u] = tinv[u] + _dot(tinv[u].astype(BF16), q[u].astype(BF16))
    a_hat, u_loc = {}, {}
    for u in units:
        tb = tinv[u].astype(BF16)
        a_hat[u] = _dot(tb, a[u]).astype(BF16)
        u_loc[u] = _dot(tb, wv[u][:n].astype(BF16)).astype(BF16)
    r_hat = {u: r[u].astype(F32) + _dot(a_rb[u], a_hat[u]) for u in units}
    y_loc = {u: wv[u][n:] + _dot(a_rb[u], u_loc[u]) for u in units}
    m_t, c_t = {}, {}
    for u in units:
        c, h = u
        gam_row = jnp.sum(jnp.where(sub == first + c, gam_ref[...], 0.0), axis=0, keepdims=True)
        bg, kg = tile(bg_ref, *u), tile(kg_ref, *u)
        m_t[u] = jnp.where(diag, gam_row[:, h * HEAD:(h + 1) * HEAD], 0.0) + _dot_tn(bg, a_hat[u])
        c_t[u] = _dot_tn(bg, u_loc[u]) + _dot_tn(kg, v[u])
    for u in units:
        c, h = u
        mr = _dot(jnp.concatenate([m_t[u], r_hat[u]], axis=0).astype(BF16), s_ref[h].astype(BF16))
        s_ref[h] = mr[:HEAD] + c_t[u]
        y_ref[c * n:(c + 1) * n, h * HEAD:(h + 1) * HEAD] = mr[HEAD:] + y_loc[u]


def _k2(at, bt, kt, rt, v, bg, kg, gam):
    t = at.shape[0]
    rows = RWKV_CHUNK * K2_CHUNKS
    per = K1_ROWS // rows
    seq = _row_spec(rows, D_RWKV)
    return pl.pallas_call(
        _k2_kernel,
        grid=(t // rows,),
        in_specs=[seq] * 7 + [pl.BlockSpec((None, 8, D_RWKV), lambda i: (i // per, 0, 0))],
        out_specs=seq,
        out_shape=jax.ShapeDtypeStruct((t, D_RWKV), F32),
        scratch_shapes=[pltpu.VMEM((N_HEADS, HEAD, HEAD), F32)],
        compiler_params=_params("arbitrary"),
        name="k2_rwkv_chunked",
    )(at, bt, kt, rt, v, bg, kg, gam)


def _tail(h, p_ref, nffn_ref, up_ref, down_ref, nple_ref, gate_ref, proj_ref):
    hn = _rms(h, nffn_ref[...]).astype(BF16)
    mlp = None
    for c in range(D_FF // FF_CHUNK):
        a = jnp.maximum(_dot(hn, up_ref[:, c * FF_CHUNK:(c + 1) * FF_CHUNK]), 0.0)
        part = _dot((a * a).astype(BF16), down_ref[c * FF_CHUNK:(c + 1) * FF_CHUNK, :])
        mlp = part if mlp is None else mlp + part
    h = h + mlp
    gate = _sigmoid(_dot(_rms(h, nple_ref[...]).astype(BF16), gate_ref[...]))
    return h + _dot(p_ref[...].astype(BF16), proj_ref[...]) * gate


def _k3_kernel(y_ref, g_ref, bv_ref, yb_ref, x_ref, p_ref, gnmean_ref, lnw_ref, lnb_ref, wout_ref,
               nffn_ref, up_ref, down_ref, nple_ref, gate_ref, proj_ref, nnext_ref, h_ref, hn_ref):
    y = y_ref[...]
    d = y - _dot_wide_lhs(y, gnmean_ref[...], 2)
    var = _dot_wide_lhs(d * d, gnmean_ref[...], 2)
    yn = d * lax.rsqrt(var + GN_EPS) * lnw_ref[...] + lnb_ref[...]
    ya = ((yn + bv_ref[...]) * g_ref[...]).astype(BF16)
    h = x_ref[...] + _dot(ya, wout_ref[0:D_RWKV, :]) + _dot(yb_ref[...], wout_ref[D_RWKV:, :])
    h = _tail(h, p_ref, nffn_ref, up_ref, down_ref, nple_ref, gate_ref, proj_ref)
    h_ref[...] = h
    hn_ref[...] = _rms(h, nnext_ref[...])


def _k3(y, g, bv, yb, x2d, p2d, ln_w, ln_b, wout, nffn, up, down, nple, gate, proj, nnext):
    t = x2d.shape[0]
    rows = TAIL_ROWS
    consts = [_head_ones(1.0 / HEAD), ln_w, ln_b, wout, nffn, up, down, nple, gate, proj, nnext]
    return pl.pallas_call(
        _k3_kernel,
        grid=(t // rows,),
        in_specs=[_row_spec(rows, D_RWKV), _row_spec(rows, D_RWKV), _row_spec(rows, D_RWKV),
                  _row_spec(rows, D_CONV), _row_spec(rows, D_MODEL), _row_spec(rows, D_PLE)]
                 + [_const_spec(c.shape) for c in consts],
        out_specs=[_row_spec(rows, D_MODEL), _row_spec(rows, D_MODEL)],
        out_shape=[jax.ShapeDtypeStruct((t, D_MODEL), F32), jax.ShapeDtypeStruct((t, D_MODEL), F32)],
        compiler_params=_params("arbitrary"),
        name="k3_mix_out_mlp_ple",
    )(y, g, bv, yb, x2d, p2d, *consts)


def _shift_rows(x, s):
    n = x.shape[0]
    if s % 8 == 0:
        return jnp.concatenate([jnp.zeros((s, x.shape[1]), x.dtype), x[:n - s]], axis=0)
    keep = lax.broadcasted_iota(jnp.int32, x.shape, 0) >= s
    return jnp.where(keep, pltpu.roll(x, s, axis=0), 0.0)


def _swap_halves(x):
    width = x.shape[1]
    is_re = (lax.broadcasted_iota(jnp.int32, x.shape, 1) // SSM_STATE) % 2 == 0
    return jnp.where(is_re, pltpu.roll(x, width - SSM_STATE, axis=1), pltpu.roll(x, SSM_STATE, axis=1))


def _k4_kernel(x_ref, tc_ref, pc_ref, qc_ref, arr_ref, ais_ref, dsk_ref, y_ref, wt_ref, carry_ref):
    nb, rows, lanes = x_ref.shape
    gl = lanes // SSM_GROUP
    sw = 2 * SSM_STATE
    zero = jnp.zeros((), BF16)

    @pl.when(pl.program_id(1) == 0)
    def _():
        carry_ref[...] = jnp.zeros_like(carry_ref)

    g_row = lax.broadcasted_iota(jnp.int32, (lanes, lanes), 0) // SSM_GROUP
    g_col = lax.broadcasted_iota(jnp.int32, (lanes, lanes), 1) // SSM_GROUP
    for j in range(nb):
        tap = jnp.concatenate([tc_ref[nb - 1 - j]] * gl, axis=0)
        wt_ref[j * lanes:(j + 1) * lanes, :] = jnp.where(g_row == g_col, tap, zero)

    p_row = lax.broadcasted_iota(jnp.int32, (lanes, gl * sw), 0) // SSM_GROUP
    p_col = lax.broadcasted_iota(jnp.int32, (lanes, gl * sw), 1) // sw
    st = None
    for s in range(nb):
        pbd = jnp.where(p_row == p_col, jnp.concatenate([pc_ref[s]] * gl, axis=1), zero)
        part = _dot(x_ref[s], pbd)
        st = part if st is None else st + part

    first = lax.broadcasted_iota(jnp.int32, st.shape, 0) == 0
    h8 = carry_ref[...]
    h_in = h8[0:1, :]
    st = st + jnp.where(first, (arr_ref[0:1, :] * h8 + ais_ref[0:1, :] * _swap_halves(h8))[0:1, :], 0.0)
    for k in range(arr_ref.shape[0]):
        if (1 << k) >= rows:
            break
        sh = _shift_rows(st, 1 << k)
        st = st + arr_ref[k:k + 1, :] * sh + ais_ref[k:k + 1, :] * _swap_halves(sh)
    carry_ref[0:1, :] = st[rows - 1:rows, :]
    hb = jnp.where(first, h_in, _shift_rows(st, 1)).astype(BF16)

    q_row = lax.broadcasted_iota(jnp.int32, (gl * sw, lanes), 0) // sw
    q_col = lax.broadcasted_iota(jnp.int32, (gl * sw, lanes), 1) // SSM_GROUP
    dsk = dsk_ref[...]
    for t in range(nb):
        xcat = jnp.concatenate([x_ref[s] for s in range(t + 1)], axis=1)
        qbd = jnp.where(q_row == q_col, jnp.concatenate([qc_ref[t]] * gl, axis=0), zero)
        y = _dot(xcat, wt_ref[(nb - 1 - t) * lanes:, :]) + _dot(hb, qbd)
        y_ref[t] = _gelu_tanh(y + dsk * x_ref[t].astype(F32)).astype(BF16)


def _k4(xs, tc, pc, qc, arr, ais, dsk):
    nb, m, d = xs.shape
    lanes = 256
    rows = min(S5_ROWS, m)
    sw = 2 * SSM_STATE * (lanes // SSM_GROUP)
    return pl.pallas_call(
        _k4_kernel,
        grid=(d // lanes, m // rows),
        in_specs=[pl.BlockSpec((nb, rows, lanes), lambda q, i: (0, i, q)),
                  pl.BlockSpec((nb, SSM_GROUP, lanes), lambda q, i: (0, 0, q)),
                  pl.BlockSpec((nb, lanes, 2 * SSM_STATE), lambda q, i: (0, q, 0)),
                  pl.BlockSpec((nb, 2 * SSM_STATE, lanes), lambda q, i: (0, 0, q)),
                  pl.BlockSpec((None,) + arr.shape[1:], lambda q, i: (q, 0, 0)),
                  pl.BlockSpec((None,) + ais.shape[1:], lambda q, i: (q, 0, 0)),
                  pl.BlockSpec((1, lanes), lambda q, i: (0, q))],
        out_specs=pl.BlockSpec((nb, rows, lanes), lambda q, i: (0, i, q)),
        out_shape=jax.ShapeDtypeStruct((nb, m, d), BF16),
        scratch_shapes=[pltpu.VMEM((nb * lanes, lanes), BF16), pltpu.VMEM((8, sw), F32)],
        compiler_params=_params("arbitrary", "arbitrary"),
        name="k4_s5",
    )(xs, tc, pc, qc, arr, ais, dsk)


def _s5_tables(lam_re, lam_im, log_step, b_re, b_im, c_re, c_im, d_skip, n_blocks):
    g, p, c, nb = SSM_GROUPS, SSM_STATE, SSM_GROUP, S5_BLOCK
    lre = jnp.minimum(lam_re.astype(F32), -1e-4)
    lim = lam_im.astype(F32)
    step = jnp.exp(log_step.astype(F32))[:, None]
    ar, ai = lre * step, lim * step
    n = jnp.arange(nb + 1, dtype=F32)[:, None, None]
    mag = jnp.exp(n * ar)
    pr, pi = mag * jnp.cos(n * ai), mag * jnp.sin(n * ai)
    nr, ni = pr[1] - 1.0, pi[1]
    den = lre * lre + lim * lim
    qr, qi = (nr * lre + ni * lim) / den, (ni * lre - nr * lim) / den
    bre, bim = b_re.astype(F32), b_im.astype(F32)
    bbr = qr[..., None] * bre - qi[..., None] * bim
    bbi = qr[..., None] * bim + qi[..., None] * bre
    mr = pr[:nb, :, :, None] * bbr - pi[:nb, :, :, None] * bbi
    mi = pr[:nb, :, :, None] * bbi + pi[:nb, :, :, None] * bbr
    cre, cim = c_re.astype(F32), c_im.astype(F32)
    taps = (jnp.einsum('gop,tgpc->gtoc', cre, mr, precision=HIGHEST)
            - jnp.einsum('gop,tgpc->gtoc', cim, mi, precision=HIGHEST))
    tc = taps.transpose(1, 3, 0, 2).reshape(nb, c, g * c)
    pc = jnp.stack([mr[::-1], mi[::-1]], axis=0).transpose(1, 2, 4, 0, 3).reshape(nb, g * c, 2 * p)
    prt, pit = pr[1:, :, None, :], pi[1:, :, None, :]
    q_re = cre[None] * prt - cim[None] * pit
    q_im = -(cre[None] * pit + cim[None] * prt)
    qc = jnp.stack([q_re, q_im], axis=1).transpose(0, 1, 4, 2, 3).reshape(nb, 2 * p, g * c)
    levels = max(1, int(np.ceil(np.log2(max(n_blocks, 2)))))
    a_re, a_im = [pr[nb]], [pi[nb]]
    for _ in range(levels - 1):
        r, i = a_re[-1], a_im[-1]
        a_re.append(r * r - i * i)
        a_im.append(2.0 * r * i)
    a_re, a_im = jnp.stack(a_re, axis=0), jnp.stack(a_im, axis=0)
    tiles = g * c // 256
    arr = jnp.stack([a_re, a_re], axis=2).reshape(levels, tiles, -1).transpose(1, 0, 2)
    ais = jnp.stack([-a_im, a_im], axis=2).reshape(levels, tiles, -1).transpose(1, 0, 2)
    return tc.astype(BF16), pc.astype(BF16), qc.astype(BF16), arr, ais, d_skip.astype(F32).reshape(1, g * c)


def _k5_kernel(yg_ref, h_ref, p_ref, w1_ref, w2_ref, nffn_ref, up_ref, down_ref, nple_ref, gate_ref,
               proj_ref, nfinal_ref, o_ref):
    yg = yg_ref[...]
    h = h_ref[...] + _dot(yg, w1_ref[...]) * _sigmoid(_dot(yg, w2_ref[...]))
    h = _tail(h, p_ref, nffn_ref, up_ref, down_ref, nple_ref, gate_ref, proj_ref)
    o_ref[...] = _rms(h, nfinal_ref[...])


def _k5(yg, h, p2d, w1, w2, nffn, up, down, nple, gate, proj, nfinal):
    t = h.shape[0]
    rows = TAIL_ROWS
    consts = [w1, w2, nffn, up, down, nple, gate, proj, nfinal]
    return pl.pallas_call(
        _k5_kernel,
        grid=(t // rows,),
        in_specs=[_row_spec(rows, D_MODEL), _row_spec(rows, D_MODEL), _row_spec(rows, D_PLE)]
                 + [_const_spec(c.shape) for c in consts],
        out_specs=_row_spec(rows, D_MODEL),
        out_shape=jax.ShapeDtypeStruct((t, D_MODEL), F32),
        compiler_params=_params("arbitrary"),
        name="k5_glu_mlp_ple_norm",
    )(yg, h, p2d, *consts)


def _row(vec):
    return vec.astype(F32).reshape(1, -1)


def _lora_rows(m, first, width):
    return jnp.zeros((D_LORA, width), F32).at[first:first + m.shape[0]].set(m.astype(F32)).astype(BF16)


def kernel(x, p, l0_norm_mix, l0_w_in, l0_shift_mu, l0_w0, l0_w_lora_up, l0_a0, l0_a_lora_up, l0_g_lora_up, l0_k_k, l0_k_a, l0_r_k, l0_ln_w, l0_ln_b, l0_conv_w, l0_w_out, l0_norm_ffn, l0_ffn_up, l0_ffn_down, l0_norm_ple, l0_ple_gate, l0_ple_proj, l1_norm_mix, l1_lambda_re, l1_lambda_im, l1_log_step, l1_b_re, l1_b_im, l1_c_re, l1_c_im, l1_d_skip, l1_glu_w1, l1_glu_w2, l1_norm_ffn, l1_ffn_up, l1_ffn_down, l1_norm_ple, l1_ple_gate, l1_ple_proj, norm_final):
    bsz, t, _ = x.shape
    assert bsz == 1 and t % K1_ROWS == 0 and t % TAIL_ROWS == 0
    x2d = x.reshape(t, D_MODEL)
    p0, p1 = p[0].reshape(t, D_PLE), p[1].reshape(t, D_PLE)

    at, bt, kt, rt, v, bg, kg, gam, g, bv, yb = _k1(
        x2d, _row(l0_norm_mix), l0_w_in.astype(BF16), _row(l0_shift_mu[:D_IN_A]), _row(l0_w0),
        _lora_rows(l0_w_lora_up, 0, D_RWKV), _row(l0_a0), _lora_rows(l0_a_lora_up, LORA_W, D_RWKV),
        _lora_rows(l0_g_lora_up, LORA_W + LORA_A, D_RWKV), _row(l0_k_k), _row(l0_k_a), _row(l0_r_k),
        l0_conv_w.astype(F32))
    y = _k2(at, bt, kt, rt, v, bg, kg, gam)
    h, hn = _k3(y, g, bv, yb, x2d, p0, _row(l0_ln_w), _row(l0_ln_b), l0_w_out.astype(BF16),
                _row(l0_norm_ffn), l0_ffn_up.astype(BF16), l0_ffn_down.astype(BF16),
                _row(l0_norm_ple), l0_ple_gate.astype(BF16), l0_ple_proj.astype(BF16), _row(l1_norm_mix))

    nblk = t // S5_BLOCK
    xs = hn.astype(BF16).reshape(nblk, S5_BLOCK, D_MODEL).transpose(1, 0, 2)
    tables = _s5_tables(l1_lambda_re, l1_lambda_im, l1_log_step, l1_b_re, l1_b_im, l1_c_re, l1_c_im,
                        l1_d_skip, nblk)
    yg = _k4(xs, *tables).transpose(1, 0, 2).reshape(t, D_MODEL)

    out = _k5(yg, h, p1, l1_glu_w1.astype(BF16), l1_glu_w2.astype(BF16), _row(l1_norm_ffn),
              l1_ffn_up.astype(BF16), l1_ffn_down.astype(BF16), _row(l1_norm_ple),
              l1_ple_gate.astype(BF16), l1_ple_proj.astype(BF16), _row(norm_final))
    return out.reshape(bsz, t, D_MODEL)
```

```python
import numpy as np
import jax
import jax.numpy as jnp
from jax import lax
from jax.experimental import pallas as pl
from jax.experimental.pallas import tpu as pltpu

F32 = jnp.float32
BF16 = jnp.bfloat16

D_MODEL = 1024
N_HEADS = 8
HEAD = 64
D_RWKV = N_HEADS * HEAD
D_CONV = D_MODEL - D_RWKV
LORA_W, LORA_A, LORA_G = 64, 64, 128
D_LORA = LORA_W + LORA_A + LORA_G
D_IN_A = 3 * D_RWKV + D_LORA
D_IN = D_IN_A + 3 * D_CONV
GN_EPS = 64e-5
RMS_EPS = 1e-6
SSM_GROUPS, SSM_GROUP, SSM_STATE = 64, 16, 64
D_FF = 4 * D_MODEL
D_PLE = 256

VMEM_LIMIT = 56 * 1024 * 1024
HIGHEST = lax.Precision.HIGHEST

RWKV_CHUNK = 64
K1_ROWS = 256
K2_CHUNKS = 2
TAIL_ROWS = 512
FF_CHUNK = 1024
S5_BLOCK = 16
S5_ROWS = 512


def _dot(a, b):
    return jnp.dot(a, b, preferred_element_type=F32)


def _dot_nt(a, b):
    return lax.dot_general(a, b, (((1,), (1,)), ((), ())), preferred_element_type=F32)


def _dot_tn(a, b):
    return lax.dot_general(a, b, (((0,), (0,)), ((), ())), preferred_element_type=F32)


def _split_bf16(x, n):
    pieces = []
    for _ in range(n - 1):
        p = x.astype(BF16)
        pieces.append(p)
        x = x - p.astype(F32)
    pieces.append(x.astype(BF16))
    return pieces


def _dot_wide_lhs(x, w, n):
    return sum(_dot(p, w) for p in _split_bf16(x, n))


def _dot_wide_rhs(w, x, n):
    return sum(_dot(w, p) for p in _split_bf16(x, n))


def _rms(x, gain):
    return x * lax.rsqrt(jnp.mean(x * x, axis=-1, keepdims=True) + RMS_EPS) * gain


def _sigmoid(x):
    return 1.0 / (1.0 + jnp.exp(-x))


def _softplus(x):
    return jnp.maximum(x, 0.0) + jnp.log(1.0 + jnp.exp(-jnp.abs(x)))


def _gelu_tanh(x):
    return 0.5 * x * (1.0 + jnp.tanh(np.sqrt(2.0 / np.pi).astype(np.float32) * (x + 0.044715 * (x * x * x))))


def _const_spec(shape):
    nd = len(shape)
    return pl.BlockSpec(shape, lambda *_: (0,) * nd, pipeline_mode=pl.Buffered(1))


def _row_spec(rows, cols):
    return pl.BlockSpec((rows, cols), lambda i: (i, 0))


def _layer_spec(rows, layer):
    return pl.BlockSpec((None, rows, D_PLE), lambda i: (layer, i, 0))


def _params(*sem):
    return pltpu.CompilerParams(dimension_semantics=sem, vmem_limit_bytes=VMEM_LIMIT)


def _head_ones(scale):
    lane = np.arange(D_RWKV)
    return jnp.asarray(scale * ((lane[:, None] // HEAD) == (lane[None, :] // HEAD)), BF16)


def _k1_kernel(x_ref, gmix_ref, win_ref, mu_ref, w0_ref, ww_ref, a0_ref, wa_ref, wg_ref, kk_ref, ka_ref,
               rk_ref, cw_ref, tri_ref, blk_ref, sel_ref, hsum_ref,
               at_ref, bt_ref, kt_ref, rt_ref, v_ref, bg_ref, kg_ref, gam_ref, g_ref, bv_ref, yb_ref,
               zs_ref, us_ref):
    rows = x_ref.shape[0]

    @pl.when(pl.program_id(0) == 0)
    def _():
        zs_ref[0:8, :] = jnp.zeros((8, D_IN_A), F32)
        us_ref[0:8, :] = jnp.zeros((8, D_CONV), F32)

    xn = _rms(x_ref[...], gmix_ref[...])
    z = _dot(xn.astype(BF16), win_ref[...])

    za = z[:, :D_IN_A]
    zs_ref[8:8 + rows, :] = za
    za_prev = zs_ref[7:7 + rows, :]
    zs_ref[7:8, :] = za[rows - 1:rows, :]
    za = za + mu_ref[...] * (za_prev - za)

    r = za[:, 0:D_RWKV]
    k = za[:, D_RWKV:2 * D_RWKV]
    v = za[:, 2 * D_RWKV:3 * D_RWKV]
    lora_in = za[:, 3 * D_RWKV:]

    w_log = -_softplus(-(w0_ref[...] + _dot(jnp.tanh(lora_in).astype(BF16), ww_ref[...]))) - 0.5
    logw = -jnp.exp(w_log)
    lr = _sigmoid(a0_ref[...] + _dot(lora_in.astype(BF16), wa_ref[...]))
    g_ref[...] = _dot(_sigmoid(lora_in).astype(BF16), wg_ref[...])

    lw = _split_bf16(logw, 3)
    c_incl = sum(_dot(tri_ref[...], p) for p in lw)
    c_tot = sum(_dot(blk_ref[...], p) for p in lw)
    gam_ref[...] = jnp.exp(sum(_dot(sel_ref[...], p) for p in lw))
    e_neg = jnp.exp(-c_incl)
    e_rest = jnp.exp(c_tot - c_incl)

    kk = k * kk_ref[...]
    kk = kk / jnp.maximum(jnp.sqrt(_dot_wide_lhs(kk * kk, hsum_ref[...], 2)), 1e-12)
    kmod = k * (1.0 + (lr - 1.0) * ka_ref[...])
    at_ref[...] = (-kk * jnp.exp(c_incl - logw)).astype(BF16)
    bt_ref[...] = (kk * lr * e_neg).astype(BF16)
    bg_ref[...] = (kk * lr * e_rest).astype(BF16)
    kt_ref[...] = (kmod * e_neg).astype(BF16)
    kg_ref[...] = (kmod * e_rest).astype(BF16)
    rt_ref[...] = (r * jnp.exp(c_incl)).astype(BF16)
    v_ref[...] = v.astype(BF16)
    bv_ref[...] = _dot_wide_lhs(r * kmod * rk_ref[...], hsum_ref[...], 2) * v

    zb = z[:, D_IN_A:]
    b_gate = zb[:, :D_CONV]
    u = zb[:, D_CONV:2 * D_CONV] * zb[:, 2 * D_CONV:]
    us_ref[8:8 + rows, :] = u
    u1 = us_ref[7:7 + rows, :]
    u2 = us_ref[6:6 + rows, :]
    us_ref[6:8, :] = u[rows - 2:rows, :]
    cw = cw_ref[...]
    yb_ref[...] = (b_gate * (cw[0:1] * u + cw[1:2] * u1 + cw[2:3] * u2)).astype(BF16)


def _k1(x2d, gmix, win, mu, w0, ww, a0, wa, wg, k_k, k_a, r_k, conv_w):
    t = x2d.shape[0]
    rows = K1_ROWS
    nchunk = rows // RWKV_CHUNK
    r_idx = np.arange(rows)
    same = (r_idx[:, None] // RWKV_CHUNK) == (r_idx[None, :] // RWKV_CHUNK)
    tri = jnp.asarray(same & (r_idx[None, :] <= r_idx[:, None]), BF16)
    blk = jnp.asarray(same, BF16)
    sel = jnp.asarray(np.arange(8)[:, None] == (r_idx[None, :] // RWKV_CHUNK), BF16)

    consts = [gmix, win, mu, w0, ww, a0, wa, wg, k_k, k_a, r_k, conv_w, tri, blk, sel, _head_ones(1.0)]
    wide = jax.ShapeDtypeStruct((t, D_RWKV), BF16)
    out_shape = [wide] * 7 + [
        jax.ShapeDtypeStruct((t // rows, 8, D_RWKV), F32),
        jax.ShapeDtypeStruct((t, D_RWKV), F32),
        jax.ShapeDtypeStruct((t, D_RWKV), F32),
        jax.ShapeDtypeStruct((t, D_CONV), BF16),
    ]
    out_specs = [_row_spec(rows, D_RWKV)] * 7 + [
        pl.BlockSpec((None, 8, D_RWKV), lambda i: (i, 0, 0)),
        _row_spec(rows, D_RWKV), _row_spec(rows, D_RWKV), _row_spec(rows, D_CONV)]
    return pl.pallas_call(
        _k1_kernel,
        grid=(t // rows,),
        in_specs=[_row_spec(rows, D_MODEL)] + [_const_spec(c.shape) for c in consts],
        out_specs=out_specs,
        out_shape=out_shape,
        scratch_shapes=[pltpu.VMEM((rows + 8, D_IN_A), F32), pltpu.VMEM((rows + 8, D_CONV), F32)],
        compiler_params=_params("arbitrary"),
        name="k1_inproj_prep",
    )(x2d, *consts)


def _k2_kernel(at_ref, bt_ref, kt_ref, rt_ref, v_ref, bg_ref, kg_ref, gam_ref, y_ref, s_ref):
    n = RWKV_CHUNK
    step = pl.program_id(0)

    @pl.when(step == 0)
    def _():
        s_ref[...] = jnp.zeros_like(s_ref)

    row = lax.broadcasted_iota(jnp.int32, (n, n), 0)
    col = lax.broadcasted_iota(jnp.int32, (n, n), 1)
    strict, incl, diag = col < row, col <= row, col == row
    eye = jnp.where(diag, 1.0, 0.0).astype(F32)
    sub = lax.broadcasted_iota(jnp.int32, (8, D_RWKV), 0)
    first = (step * K2_CHUNKS) % (K1_ROWS // n)

    units = [(c, h) for c in range(K2_CHUNKS) for h in range(N_HEADS)]

    def tile(ref, c, h):
        return ref[c * n:(c + 1) * n, h * HEAD:(h + 1) * HEAD]

    row2 = lax.broadcasted_iota(jnp.int32, (2 * n, 2 * n), 0)
    col2 = lax.broadcasted_iota(jnp.int32, (2 * n, 2 * n), 1) % n
    tri2 = col2 <= jnp.where(row2 < n, row2 - 1, row2 - n)
    zeros = jnp.zeros((n, HEAD), BF16)

    a = {u: tile(at_ref, *u) for u in units}
    r = {u: tile(rt_ref, *u) for u in units}
    v = {u: tile(v_ref, *u) for u in units}
    gm = {}
    for u in units:
        ar = jnp.concatenate([a[u], r[u]], axis=0)
        bk = jnp.concatenate([tile(bt_ref, *u), tile(kt_ref, *u)], axis=0)
        gm[u] = jnp.where(tri2, _dot_nt(ar, bk), 0.0)
    gmb = {u: gm[u].astype(BF16) for u in units}
    wv = {u: _dot(gmb[u], jnp.concatenate([zeros, v[u]], axis=0)) for u in units}

    q = {u: gm[u][:n, :n] for u in units}
    tinv = {u: eye + q[u] for u in units}
    for u in units:
        qb = q[u].astype(BF16)
        q[u] = _dot(qb, qb)
    for level in range(1, 6):
        for u in units:
            qb = q[u].astype(BF16)
            if level < 5:
                tq = _dot(jnp.concatenate([tinv[u].astype(BF16), qb], axis=0), qb)
                tinv[u], q[u] = tinv[u] + tq[:n], tq[n:]
            else:
                tinv[u] = tinv[u] + _dot(tinv[u].astype(BF16), qb)

    au, ry, mc = {}, {}, {}
    for u in units:
        rhs = jnp.concatenate([a[u], wv[u][:n].astype(BF16)], axis=1)
        au[u] = _dot(tinv[u].astype(BF16), rhs).astype(BF16)
    for u in units:
        rhs = jnp.concatenate([au[u], jnp.zeros_like(au[u])], axis=0)
        ry[u] = jnp.concatenate([r[u].astype(F32), wv[u][n:]], axis=1) + _dot(gmb[u][n:], rhs)
    for u in units:
        lhs = jnp.concatenate([tile(bg_ref, *u), tile(kg_ref, *u)], axis=0)
        rhs = jnp.concatenate([au[u], jnp.concatenate([zeros, v[u]], axis=1)], axis=0)
        mc[u] = _dot_tn(lhs, rhs)
    for u in units:
        c, h = u
        gam_row = jnp.sum(jnp.where(sub == first + c, gam_ref[...], 0.0), axis=0, keepdims=True)
        m_t = jnp.where(diag, gam_row[:, h * HEAD:(h + 1) * HEAD], 0.0) + mc[u][:, :HEAD]
        mr = _dot(jnp.concatenate([m_t, ry[u][:, :HEAD]], axis=0).astype(BF16), s_ref[h].astype(BF16))
        s_ref[h] = mr[:HEAD] + mc[u][:, HEAD:]
        y_ref[c * n:(c + 1) * n, h * HEAD:(h + 1) * HEAD] = mr[HEAD:] + ry[u][:, HEAD:]


def _k2(at, bt, kt, rt, v, bg, kg, gam):
    t = at.shape[0]
    rows = RWKV_CHUNK * K2_CHUNKS
    per = K1_ROWS // rows
    seq = _row_spec(rows, D_RWKV)
    return pl.pallas_call(
        _k2_kernel,
        grid=(t // rows,),
        in_specs=[seq] * 7 + [pl.BlockSpec((None, 8, D_RWKV), lambda i: (i // per, 0, 0))],
        out_specs=seq,
        out_shape=jax.ShapeDtypeStruct((t, D_RWKV), F32),
        scratch_shapes=[pltpu.VMEM((N_HEADS, HEAD, HEAD), F32)],
        compiler_params=_params("arbitrary"),
        name="k2_rwkv_chunked",
    )(at, bt, kt, rt, v, bg, kg, gam)


def _tail(h, p_ref, nffn_ref, up_ref, down_ref, nple_ref, gate_ref, proj_ref):
    hn = _rms(h, nffn_ref[...]).astype(BF16)
    mlp = None
    for c in range(D_FF // FF_CHUNK):
        a = jnp.maximum(_dot(hn, up_ref[:, c * FF_CHUNK:(c + 1) * FF_CHUNK]), 0.0)
        part = _dot((a * a).astype(BF16), down_ref[c * FF_CHUNK:(c + 1) * FF_CHUNK, :])
        mlp = part if mlp is None else mlp + part
    h = h + mlp
    gate = _sigmoid(_dot(_rms(h, nple_ref[...]).astype(BF16), gate_ref[...]))
    return h + _dot(p_ref[...].astype(BF16), proj_ref[...]) * gate


def _k3_kernel(y_ref, g_ref, bv_ref, yb_ref, x_ref, p_ref, gnmean_ref, lnw_ref, lnb_ref, wout_ref,
               nffn_ref, up_ref, down_ref, nple_ref, gate_ref, proj_ref, nnext_ref, h_ref, hn_ref):
    y = y_ref[...]
    d = y - _dot_wide_lhs(y, gnmean_ref[...], 2)
    var = _dot_wide_lhs(d * d, gnmean_ref[...], 2)
    yn = d * lax.rsqrt(var + GN_EPS) * lnw_ref[...] + lnb_ref[...]
    ya = ((yn + bv_ref[...]) * g_ref[...]).astype(BF16)
    h = x_ref[...] + _dot(ya, wout_ref[0:D_RWKV, :]) + _dot(yb_ref[...], wout_ref[D_RWKV:, :])
    h = _tail(h, p_ref, nffn_ref, up_ref, down_ref, nple_ref, gate_ref, proj_ref)
    h_ref[...] = h
    hn_ref[...] = _rms(h, nnext_ref[...])


def _k3(y, g, bv, yb, x2d, p3d, ln_w, ln_b, wout, nffn, up, down, nple, gate, proj, nnext):
    t = x2d.shape[0]
    rows = TAIL_ROWS
    consts = [_head_ones(1.0 / HEAD), ln_w, ln_b, wout, nffn, up, down, nple, gate, proj, nnext]
    return pl.pallas_call(
        _k3_kernel,
        grid=(t // rows,),
        in_specs=[_row_spec(rows, D_RWKV), _row_spec(rows, D_RWKV), _row_spec(rows, D_RWKV),
                  _row_spec(rows, D_CONV), _row_spec(rows, D_MODEL), _layer_spec(rows, 0)]
                 + [_const_spec(c.shape) for c in consts],
        out_specs=[_row_spec(rows, D_MODEL), _row_spec(rows, D_MODEL)],
        out_shape=[jax.ShapeDtypeStruct((t, D_MODEL), F32), jax.ShapeDtypeStruct((t, D_MODEL), F32)],
        compiler_params=_params("arbitrary"),
        name="k3_mix_out_mlp_ple",
    )(y, g, bv, yb, x2d, p3d, *consts)


def _shift_rows(x, s):
    n = x.shape[0]
    if s % 8 == 0:
        return jnp.concatenate([jnp.zeros((s, x.shape[1]), x.dtype), x[:n - s]], axis=0)
    keep = lax.broadcasted_iota(jnp.int32, x.shape, 0) >= s
    return jnp.where(keep, pltpu.roll(x, s, axis=0), 0.0)


def _swap_halves(x):
    width = x.shape[1]
    is_re = (lax.broadcasted_iota(jnp.int32, x.shape, 1) // SSM_STATE) % 2 == 0
    return jnp.where(is_re, pltpu.roll(x, width - SSM_STATE, axis=1), pltpu.roll(x, SSM_STATE, axis=1))


def _k4_kernel(x_ref, c0_ref, pc_ref, qc_ref, arr_ref, ais_ref, dsk_ref, y_ref, wt_ref, carry_ref):
    nb, rows, lanes = x_ref.shape
    gl = lanes // SSM_GROUP
    sw = 2 * SSM_STATE
    zero = jnp.zeros((), BF16)

    pw = 2 * SSM_GROUP
    npair = lanes // pw

    @pl.when(pl.program_id(1) == 0)
    def _():
        carry_ref[...] = jnp.zeros_like(carry_ref)
        c_row = lax.broadcasted_iota(jnp.int32, (gl * sw, lanes), 0) // sw
        c_col = lax.broadcasted_iota(jnp.int32, (gl * sw, lanes), 1) // SSM_GROUP
        cbd = jnp.where(c_row == c_col, jnp.concatenate([c0_ref[...]] * gl, axis=0), zero)
        b_row = lax.broadcasted_iota(jnp.int32, (lanes, gl * sw), 0) // SSM_GROUP
        b_col = lax.broadcasted_iota(jnp.int32, (lanes, gl * sw), 1) // sw
        for s in range(nb):
            lam_b = jnp.concatenate([pc_ref[j][s * pw:(s + 1) * pw, :] for j in range(npair)], axis=0)
            bbd = jnp.where(b_row == b_col, jnp.concatenate([lam_b] * gl, axis=1), zero)
            wt_ref[s * lanes:(s + 1) * lanes, :] = _dot(bbd, cbd).astype(BF16)

    slot = lax.broadcasted_iota(jnp.int32, (1, 128), 1) // pw

    def gather(src_of, src_slot_of, count):
        cols = []
        for w in range(count // 4):
            col = None
            for kk in range(4):
                src = src_of(4 * w + kk)
                shift = (pw * (kk - src_slot_of(4 * w + kk))) % 128
                src = pltpu.roll(src, shift, axis=1) if shift else src
                col = src if col is None else jnp.where(slot == kk, src, col)
            cols.append(col)
        return jnp.concatenate(cols, axis=1)

    p_row = (lax.broadcasted_iota(jnp.int32, (nb * pw, 2 * sw), 0) // SSM_GROUP) % 2
    p_col = lax.broadcasted_iota(jnp.int32, (nb * pw, 2 * sw), 1) // sw
    parts = []
    for j in range(npair):
        lhs = gather(lambda s: x_ref[s][:, 128 * (j // 4):128 * (j // 4 + 1)], lambda s: j % 4, nb)
        pbd = jnp.where(p_row == p_col, jnp.concatenate([pc_ref[j]] * 2, axis=1), zero)
        parts.append(_dot(lhs, pbd))
    st = jnp.concatenate(parts, axis=1)

    first = lax.broadcasted_iota(jnp.int32, st.shape, 0) == 0
    h8 = carry_ref[...]
    h_in = h8[0:1, :]
    st = st + jnp.where(first, (arr_ref[0:1, :] * h8 + ais_ref[0:1, :] * _swap_halves(h8))[0:1, :], 0.0)
    for k in range(arr_ref.shape[0]):
        if (1 << k) >= rows:
            break
        sh = _shift_rows(st, 1 << k)
        st = st + arr_ref[k:k + 1, :] * sh + ais_ref[k:k + 1, :] * _swap_halves(sh)
    carry_ref[0:1, :] = st[rows - 1:rows, :]
    hb = jnp.where(first, h_in, _shift_rows(st, 1)).astype(BF16)

    q_row = lax.broadcasted_iota(jnp.int32, (2 * sw, nb * pw), 0) // sw
    q_col = (lax.broadcasted_iota(jnp.int32, (2 * sw, nb * pw), 1) // SSM_GROUP) % 2
    y_in = []
    for j in range(npair):
        qbd = jnp.where(q_row == q_col, jnp.concatenate([qc_ref[j]] * 2, axis=0), zero)
        y_in.append(_dot(hb[:, j * 2 * sw:(j + 1) * 2 * sw], qbd).astype(BF16))
    dsk = dsk_ref[...]
    for t in range(nb):
        xcat = jnp.concatenate([x_ref[s] for s in range(t + 1)], axis=1)
        y_t = gather(lambda j: y_in[j][:, 128 * (t // 4):128 * (t // 4 + 1)], lambda j: t % 4, npair)
        y = _dot(xcat, wt_ref[(nb - 1 - t) * lanes:, :]) + y_t.astype(F32)
        y_ref[t] = _gelu_tanh(y + dsk * x_ref[t].astype(F32)).astype(BF16)


def _k4(xs, c0, pc, qc, arr, ais, dsk):
    nb, m, d = xs.shape
    lanes = 256
    rows = min(S5_ROWS, m)
    sw = 2 * SSM_STATE * (lanes // SSM_GROUP)
    return pl.pallas_call(
        _k4_kernel,
        grid=(d // lanes, m // rows),
        in_specs=[pl.BlockSpec((nb, rows, lanes), lambda q, i: (0, i, q)),
                  pl.BlockSpec((None,) + c0.shape[1:], lambda q, i: (q, 0, 0)),
                  pl.BlockSpec((None,) + pc.shape[1:], lambda q, i: (q, 0, 0, 0)),
                  pl.BlockSpec((None,) + qc.shape[1:], lambda q, i: (q, 0, 0, 0)),
                  pl.BlockSpec((None,) + arr.shape[1:], lambda q, i: (q, 0, 0)),
                  pl.BlockSpec((None,) + ais.shape[1:], lambda q, i: (q, 0, 0)),
                  pl.BlockSpec((1, lanes), lambda q, i: (0, q))],
        out_specs=pl.BlockSpec((nb, rows, lanes), lambda q, i: (0, i, q)),
        out_shape=jax.ShapeDtypeStruct((nb, m, d), BF16),
        scratch_shapes=[pltpu.VMEM((nb * lanes, lanes), BF16), pltpu.VMEM((8, sw), F32)],
        compiler_params=_params("arbitrary", "arbitrary"),
        name="k4_s5",
    )(xs, c0, pc, qc, arr, ais, dsk)


def _s5_tables(lam_re, lam_im, log_step, b_re, b_im, c_re, c_im, d_skip, n_blocks):
    g, p, c, nb = SSM_GROUPS, SSM_STATE, SSM_GROUP, S5_BLOCK
    lre = jnp.minimum(lam_re.astype(F32), -1e-4)
    lim = lam_im.astype(F32)
    step = jnp.exp(log_step.astype(F32))[:, None]
    ar, ai = lre * step, lim * step
    n = jnp.arange(nb + 1, dtype=F32)[:, None, None]
    mag = jnp.exp(n * ar)
    pr, pi = mag * jnp.cos(n * ai), mag * jnp.sin(n * ai)
    nr, ni = pr[1] - 1.0, pi[1]
    den = lre * lre + lim * lim
    qr, qi = (nr * lre + ni * lim) / den, (ni * lre - nr * lim) / den
    bre, bim = b_re.astype(F32), b_im.astype(F32)
    bbr = qr[..., None] * bre - qi[..., None] * bim
    bbi = qr[..., None] * bim + qi[..., None] * bre
    cre, cim = c_re.astype(F32), c_im.astype(F32)
    tiles, pairs = g * c // 256, 256 // (2 * c)
    split = lambda m: m.reshape((tiles, pairs, 2) + m.shape[1:])
    bt_r, bt_i = bbr.transpose(0, 2, 1), bbi.transpose(0, 2, 1)
    b_for_re = split(jnp.stack([bt_r, bt_i], axis=2))[:, :, None]
    b_for_im = split(jnp.stack([-bt_i, bt_r], axis=2))[:, :, None]
    ps_r = split(pr[nb - 1::-1].transpose(1, 0, 2)).transpose(0, 1, 3, 2, 4)[..., None, None, :]
    ps_i = split(pi[nb - 1::-1].transpose(1, 0, 2)).transpose(0, 1, 3, 2, 4)[..., None, None, :]
    pc = (ps_r * b_for_re + ps_i * b_for_im).reshape(tiles, pairs, nb * 2 * c, 2 * p)
    ct_r, ct_i = cre.transpose(0, 2, 1), cim.transpose(0, 2, 1)
    c0 = jnp.stack([ct_r, -ct_i], axis=1).reshape(tiles, 256 // c, 2 * p, c)
    c0 = c0.transpose(0, 2, 1, 3).reshape(tiles, 2 * p, 256)
    c_for_re = split(jnp.stack([ct_r, -ct_i], axis=1)).transpose(0, 1, 3, 4, 2, 5)[:, :, :, :, None]
    c_for_im = split(jnp.stack([-ct_i, -ct_r], axis=1)).transpose(0, 1, 3, 4, 2, 5)[:, :, :, :, None]
    pt_r = split(pr[1:].transpose(1, 2, 0)).transpose(0, 1, 3, 4, 2)[:, :, None, :, :, :, None]
    pt_i = split(pi[1:].transpose(1, 2, 0)).transpose(0, 1, 3, 4, 2)[:, :, None, :, :, :, None]
    qc = (pt_r * c_for_re + pt_i * c_for_im).reshape(tiles, pairs, 2 * p, nb * 2 * c)
    levels = max(1, int(np.ceil(np.log2(max(n_blocks, 2)))))
    a_re, a_im = [pr[nb]], [pi[nb]]
    for _ in range(levels - 1):
        r, i = a_re[-1], a_im[-1]
        a_re.append(r * r - i * i)
        a_im.append(2.0 * r * i)
    a_re, a_im = jnp.stack(a_re, axis=0), jnp.stack(a_im, axis=0)
    arr = jnp.stack([a_re, a_re], axis=2).reshape(levels, tiles, -1).transpose(1, 0, 2)
    ais = jnp.stack([-a_im, a_im], axis=2).reshape(levels, tiles, -1).transpose(1, 0, 2)
    return c0.astype(BF16), pc.astype(BF16), qc.astype(BF16), arr, ais, d_skip.astype(F32).reshape(1, g * c)


def _k5_kernel(yg_ref, h_ref, p_ref, w1_ref, w2_ref, nffn_ref, up_ref, down_ref, nple_ref, gate_ref,
               proj_ref, nfinal_ref, o_ref):
    yg = yg_ref[...]
    h = h_ref[...] + _dot(yg, w1_ref[...]) * _sigmoid(_dot(yg, w2_ref[...]))
    h = _tail(h, p_ref, nffn_ref, up_ref, down_ref, nple_ref, gate_ref, proj_ref)
    o_ref[...] = _rms(h, nfinal_ref[...])


def _k5(yg, h, p3d, w1, w2, nffn, up, down, nple, gate, proj, nfinal):
    t = h.shape[0]
    rows = TAIL_ROWS
    consts = [w1, w2, nffn, up, down, nple, gate, proj, nfinal]
    return pl.pallas_call(
        _k5_kernel,
        grid=(t // rows,),
        in_specs=[_row_spec(rows, D_MODEL), _row_spec(rows, D_MODEL), _layer_spec(rows, 1)]
                 + [_const_spec(c.shape) for c in consts],
        out_specs=_row_spec(rows, D_MODEL),
        out_shape=jax.ShapeDtypeStruct((t, D_MODEL), F32),
        compiler_params=_params("arbitrary"),
        name="k5_glu_mlp_ple_norm",
    )(yg, h, p3d, *consts)


def _row(vec):
    return vec.astype(F32).reshape(1, -1)


def _lora_rows(m, first, width):
    return jnp.zeros((D_LORA, width), F32).at[first:first + m.shape[0]].set(m.astype(F32)).astype(BF16)


def kernel(x, p, l0_norm_mix, l0_w_in, l0_shift_mu, l0_w0, l0_w_lora_up, l0_a0, l0_a_lora_up, l0_g_lora_up, l0_k_k, l0_k_a, l0_r_k, l0_ln_w, l0_ln_b, l0_conv_w, l0_w_out, l0_norm_ffn, l0_ffn_up, l0_ffn_down, l0_norm_ple, l0_ple_gate, l0_ple_proj, l1_norm_mix, l1_lambda_re, l1_lambda_im, l1_log_step, l1_b_re, l1_b_im, l1_c_re, l1_c_im, l1_d_skip, l1_glu_w1, l1_glu_w2, l1_norm_ffn, l1_ffn_up, l1_ffn_down, l1_norm_ple, l1_ple_gate, l1_ple_proj, norm_final):
    bsz, t, _ = x.shape
    assert bsz == 1 and t % K1_ROWS == 0 and t % TAIL_ROWS == 0
    x2d = x.reshape(t, D_MODEL)
    p3d = p.reshape(p.shape[0], t, D_PLE)

    at, bt, kt, rt, v, bg, kg, gam, g, bv, yb = _k1(
        x2d, _row(l0_norm_mix), l0_w_in.astype(BF16), _row(l0_shift_mu[:D_IN_A]), _row(l0_w0),
        _lora_rows(l0_w_lora_up, 0, D_RWKV), _row(l0_a0), _lora_rows(l0_a_lora_up, LORA_W, D_RWKV),
        _lora_rows(l0_g_lora_up, LORA_W + LORA_A, D_RWKV), _row(l0_k_k), _row(l0_k_a), _row(l0_r_k),
        l0_conv_w.astype(F32))
    y = _k2(at, bt, kt, rt, v, bg, kg, gam)
    h, hn = _k3(y, g, bv, yb, x2d, p3d, _row(l0_ln_w), _row(l0_ln_b), l0_w_out.astype(BF16),
                _row(l0_norm_ffn), l0_ffn_up.astype(BF16), l0_ffn_down.astype(BF16),
                _row(l0_norm_ple), l0_ple_gate.astype(BF16), l0_ple_proj.astype(BF16), _row(l1_norm_mix))

    nblk = t // S5_BLOCK
    xs = hn.astype(BF16).reshape(nblk, S5_BLOCK, D_MODEL).transpose(1, 0, 2)
    tables = _s5_tables(l1_lambda_re, l1_lambda_im, l1_log_step, l1_b_re, l1_b_im, l1_c_re, l1_c_im,
                        l1_d_skip, nblk)
    yg = _k4(xs, *tables).transpose(1, 0, 2).reshape(t, D_MODEL)

    out = _k5(yg, h, p3d, l1_glu_w1.astype(BF16), l1_glu_w2.astype(BF16), _row(l1_norm_ffn),
              l1_ffn_up.astype(BF16), l1_ffn_down.astype(BF16), _row(l1_norm_ple),
              l1_ple_gate.astype(BF16), l1_ple_proj.astype(BF16), _row(norm_final))
    return out.reshape(bsz, t, D_MODEL)
```

```python
import numpy as np
import jax
import jax.numpy as jnp
from jax import lax
from jax.experimental import pallas as pl
from jax.experimental.pallas import tpu as pltpu

F32 = jnp.float32
BF16 = jnp.bfloat16

D_MODEL = 1024
N_HEADS = 8
HEAD = 64
D_RWKV = N_HEADS * HEAD
D_CONV = D_MODEL - D_RWKV
LORA_W, LORA_A, LORA_G = 64, 64, 128
D_LORA = LORA_W + LORA_A + LORA_G
D_IN_A = 3 * D_RWKV + D_LORA
D_IN = D_IN_A + 3 * D_CONV
GN_EPS = 64e-5
RMS_EPS = 1e-6
SSM_GROUPS, SSM_GROUP, SSM_STATE = 64, 16, 64
D_FF = 4 * D_MODEL
D_PLE = 256

VMEM_LIMIT = 56 * 1024 * 1024
HIGHEST = lax.Precision.HIGHEST

RWKV_CHUNK = 64
K1_ROWS = 256
K2_CHUNKS = 4
TAIL_ROWS = 512
FF_CHUNK = 1024
S5_BLOCK = 16
S5_ROWS = 512


def _dot(a, b):
    return jnp.dot(a, b, preferred_element_type=F32)


def _dot_nt(a, b):
    return lax.dot_general(a, b, (((1,), (1,)), ((), ())), preferred_element_type=F32)


def _dot_tn(a, b):
    return lax.dot_general(a, b, (((0,), (0,)), ((), ())), preferred_element_type=F32)


def _split_bf16(x, n):
    pieces = []
    for _ in range(n - 1):
        p = x.astype(BF16)
        pieces.append(p)
        x = x - p.astype(F32)
    pieces.append(x.astype(BF16))
    return pieces


def _dot_wide_lhs(x, w, n):
    return sum(_dot(p, w) for p in _split_bf16(x, n))


def _dot_wide_rhs(w, x, n):
    return sum(_dot(w, p) for p in _split_bf16(x, n))


def _rms(x, gain):
    return x * lax.rsqrt(jnp.mean(x * x, axis=-1, keepdims=True) + RMS_EPS) * gain


def _sigmoid(x):
    return 1.0 / (1.0 + jnp.exp(-x))


def _softplus(x):
    return jnp.maximum(x, 0.0) + jnp.log(1.0 + jnp.exp(-jnp.abs(x)))


def _gelu_tanh(x):
    return 0.5 * x * (1.0 + jnp.tanh(np.sqrt(2.0 / np.pi).astype(np.float32) * (x + 0.044715 * (x * x * x))))


def _const_spec(shape):
    nd = len(shape)
    return pl.BlockSpec(shape, lambda *_: (0,) * nd, pipeline_mode=pl.Buffered(1))


def _row_spec(rows, cols):
    return pl.BlockSpec((rows, cols), lambda i: (i, 0))


def _layer_spec(rows, layer):
    return pl.BlockSpec((None, rows, D_PLE), lambda i: (layer, i, 0))


def _posmajor_spec(rows):
    return pl.BlockSpec((S5_BLOCK, rows // S5_BLOCK, D_MODEL), lambda i: (0, i, 0))


def _lane_chunks(width):
    return [slice(c, c + 128) for c in range(0, width, 128)]


def _params(*sem):
    return pltpu.CompilerParams(dimension_semantics=sem, vmem_limit_bytes=VMEM_LIMIT)


def _head_ones(scale):
    lane = np.arange(D_RWKV)
    return jnp.asarray(scale * ((lane[:, None] // HEAD) == (lane[None, :] // HEAD)), BF16)


def _k1_kernel(x_ref, gmix_ref, win_ref, mu_ref, w0_ref, ww_ref, a0_ref, wa_ref, wg_ref, kk_ref, ka_ref,
               rk_ref, cw_ref, tri_ref, sel_ref, hsum_ref,
               at_ref, bt_ref, kt_ref, rt_ref, v_ref, bg_ref, kg_ref, gam_ref, g_ref, bv_ref, yb_ref,
               zs_ref, us_ref):
    rows = x_ref.shape[0]

    @pl.when(pl.program_id(0) == 0)
    def _():
        zs_ref[0:8, :] = jnp.zeros((8, D_IN_A), F32)
        us_ref[0:8, :] = jnp.zeros((8, D_CONV), F32)

    xn = _rms(x_ref[...], gmix_ref[...])
    z = _dot(xn.astype(BF16), win_ref[...])

    za = z[:, :D_IN_A]
    zs_ref[8:8 + rows, :] = za
    za_prev = zs_ref[7:7 + rows, :]
    zs_ref[7:8, :] = za[rows - 1:rows, :]
    za = za + mu_ref[...] * (za_prev - za)

    r = za[:, 0:D_RWKV]
    k = za[:, D_RWKV:2 * D_RWKV]
    v = za[:, 2 * D_RWKV:3 * D_RWKV]
    lora_in = za[:, 3 * D_RWKV:]

    w_log = -_softplus(-(w0_ref[...] + _dot(jnp.tanh(lora_in).astype(BF16), ww_ref[...]))) - 0.5
    logw = -jnp.exp(w_log)
    lr = _sigmoid(a0_ref[...] + _dot(lora_in.astype(BF16), wa_ref[...]))
    g_ref[...] = _dot(_sigmoid(lora_in).astype(BF16), wg_ref[...])

    lw = _split_bf16(logw, 3)
    c_incl = sum(_dot(tri_ref[...], p) for p in lw[:2])
    c_chunk = sum(_dot(sel_ref[...], p) for p in lw)
    gam_ref[...] = jnp.exp(c_chunk)
    chunk_id = lax.broadcasted_iota(jnp.int32, (rows, 1), 0) // RWKV_CHUNK
    c_tot = c_chunk[0:1, :]
    for i in range(1, rows // RWKV_CHUNK):
        c_tot = jnp.where(chunk_id == i, c_chunk[i:i + 1, :], c_tot)
    e_neg = jnp.exp(-c_incl)
    e_rest = jnp.exp(c_tot - c_incl)

    kk = k * kk_ref[...]
    kk = kk / jnp.maximum(jnp.sqrt(_dot((kk * kk).astype(BF16), hsum_ref[...])), 1e-12)
    kmod = k * (1.0 + (lr - 1.0) * ka_ref[...])
    at_ref[...] = (-kk * jnp.exp(c_incl - logw)).astype(BF16)
    bt_ref[...] = (kk * lr * e_neg).astype(BF16)
    bg_ref[...] = (kk * lr * e_rest).astype(BF16)
    kt_ref[...] = (kmod * e_neg).astype(BF16)
    kg_ref[...] = (kmod * e_rest).astype(BF16)
    rt_ref[...] = (r * jnp.exp(c_incl)).astype(BF16)
    v_ref[...] = v.astype(BF16)
    bv_ref[...] = _dot((r * kmod * rk_ref[...]).astype(BF16), hsum_ref[...]) * v

    zb = z[:, D_IN_A:]
    b_gate = zb[:, :D_CONV]
    u = zb[:, D_CONV:2 * D_CONV] * zb[:, 2 * D_CONV:]
    us_ref[8:8 + rows, :] = u
    u1 = us_ref[7:7 + rows, :]
    u2 = us_ref[6:6 + rows, :]
    us_ref[6:8, :] = u[rows - 2:rows, :]
    cw = cw_ref[...]
    yb_ref[...] = (b_gate * (cw[0:1] * u + cw[1:2] * u1 + cw[2:3] * u2)).astype(BF16)


def _k1(x2d, gmix, win, mu, w0, ww, a0, wa, wg, k_k, k_a, r_k, conv_w):
    t = x2d.shape[0]
    rows = K1_ROWS
    nchunk = rows // RWKV_CHUNK
    r_idx = np.arange(rows)
    same = (r_idx[:, None] // RWKV_CHUNK) == (r_idx[None, :] // RWKV_CHUNK)
    tri = jnp.asarray(same & (r_idx[None, :] <= r_idx[:, None]), BF16)
    sel = jnp.asarray(np.arange(8)[:, None] == (r_idx[None, :] // RWKV_CHUNK), BF16)

    consts = [gmix, win, mu, w0, ww, a0, wa, wg, k_k, k_a, r_k, conv_w, tri, sel, _head_ones(1.0)]
    wide = jax.ShapeDtypeStruct((t, D_RWKV), BF16)
    out_shape = [wide] * 7 + [
        jax.ShapeDtypeStruct((t // rows, 8, D_RWKV), F32),
        jax.ShapeDtypeStruct((t, D_RWKV), F32),
        jax.ShapeDtypeStruct((t, D_RWKV), F32),
        jax.ShapeDtypeStruct((t, D_CONV), BF16),
    ]
    out_specs = [_row_spec(rows, D_RWKV)] * 7 + [
        pl.BlockSpec((None, 8, D_RWKV), lambda i: (i, 0, 0)),
        _row_spec(rows, D_RWKV), _row_spec(rows, D_RWKV), _row_spec(rows, D_CONV)]
    return pl.pallas_call(
        _k1_kernel,
        grid=(t // rows,),
        in_specs=[_row_spec(rows, D_MODEL)] + [_const_spec(c.shape) for c in consts],
        out_specs=out_specs,
        out_shape=out_shape,
        scratch_shapes=[pltpu.VMEM((rows + 8, D_IN_A), F32), pltpu.VMEM((rows + 8, D_CONV), F32)],
        compiler_params=_params("arbitrary"),
        name="k1_inproj_prep",
    )(x2d, *consts)


def _k2_kernel(at_ref, bt_ref, kt_ref, rt_ref, v_ref, bg_ref, kg_ref, gam_ref, y_ref, s_ref):
    n = RWKV_CHUNK
    step = pl.program_id(0)

    @pl.when(step == 0)
    def _():
        s_ref[...] = jnp.zeros_like(s_ref)

    row = lax.broadcasted_iota(jnp.int32, (n, n), 0)
    col = lax.broadcasted_iota(jnp.int32, (n, n), 1)
    strict, incl, diag = col < row, col <= row, col == row
    eye = jnp.where(diag, 1.0, 0.0).astype(F32)
    sub = lax.broadcasted_iota(jnp.int32, (8, D_RWKV), 0)
    first = (step * K2_CHUNKS) % (K1_ROWS // n)

    units = [(c, h) for c in range(K2_CHUNKS) for h in range(N_HEADS)]

    def tile(ref, c, h):
        return ref[c * n:(c + 1) * n, h * HEAD:(h + 1) * HEAD]

    row2 = lax.broadcasted_iota(jnp.int32, (2 * n, 2 * n), 0)
    col2 = lax.broadcasted_iota(jnp.int32, (2 * n, 2 * n), 1) % n
    tri2 = col2 <= jnp.where(row2 < n, row2 - 1, row2 - n)
    zeros = jnp.zeros((n, HEAD), BF16)

    a = {u: tile(at_ref, *u) for u in units}
    r = {u: tile(rt_ref, *u) for u in units}
    v = {u: tile(v_ref, *u) for u in units}
    gm = {}
    for u in units:
        ar = jnp.concatenate([a[u], r[u]], axis=0)
        bk = jnp.concatenate([tile(bt_ref, *u), tile(kt_ref, *u)], axis=0)
        gm[u] = jnp.where(tri2, _dot_nt(ar, bk), 0.0)
    gmb = {u: gm[u].astype(BF16) for u in units}
    wv = {u: _dot(gmb[u], jnp.concatenate([zeros, v[u]], axis=0)) for u in units}

    q = {u: gm[u][:n, :n] for u in units}
    tinv = {u: eye + q[u] for u in units}
    for u in units:
        qb = q[u].astype(BF16)
        q[u] = _dot(qb, qb)
    for level in range(1, 6):
        for u in units:
            qb = q[u].astype(BF16)
            if level < 5:
                tq = _dot(jnp.concatenate([tinv[u].astype(BF16), qb], axis=0), qb)
                tinv[u], q[u] = tinv[u] + tq[:n], tq[n:]
            else:
                tinv[u] = tinv[u] + _dot(tinv[u].astype(BF16), qb)

    au, ry, mc = {}, {}, {}
    for u in units:
        rhs = jnp.concatenate([a[u], wv[u][:n].astype(BF16)], axis=1)
        au[u] = _dot(tinv[u].astype(BF16), rhs).astype(BF16)
    for u in units:
        rhs = jnp.concatenate([au[u], jnp.zeros_like(au[u])], axis=0)
        ry[u] = jnp.concatenate([r[u].astype(F32), wv[u][n:]], axis=1) + _dot(gmb[u][n:], rhs)
    for u in units:
        lhs = jnp.concatenate([tile(bg_ref, *u), tile(kg_ref, *u)], axis=0)
        rhs = jnp.concatenate([au[u], jnp.concatenate([zeros, v[u]], axis=1)], axis=0)
        mc[u] = _dot_tn(lhs, rhs)
    for u in units:
        c, h = u
        gam_row = jnp.sum(jnp.where(sub == first + c, gam_ref[...], 0.0), axis=0, keepdims=True)
        m_t = jnp.where(diag, gam_row[:, h * HEAD:(h + 1) * HEAD], 0.0) + mc[u][:, :HEAD]
        mr = _dot(jnp.concatenate([m_t, ry[u][:, :HEAD]], axis=0).astype(BF16), s_ref[h].astype(BF16))
        s_ref[h] = mr[:HEAD] + mc[u][:, HEAD:]
        y_ref[c * n:(c + 1) * n, h * HEAD:(h + 1) * HEAD] = mr[HEAD:] + ry[u][:, HEAD:]


def _k2(at, bt, kt, rt, v, bg, kg, gam):
    t = at.shape[0]
    rows = RWKV_CHUNK * K2_CHUNKS
    per = K1_ROWS // rows
    seq = _row_spec(rows, D_RWKV)
    return pl.pallas_call(
        _k2_kernel,
        grid=(t // rows,),
        in_specs=[seq] * 7 + [pl.BlockSpec((None, 8, D_RWKV), lambda i: (i // per, 0, 0))],
        out_specs=seq,
        out_shape=jax.ShapeDtypeStruct((t, D_RWKV), F32),
        scratch_shapes=[pltpu.VMEM((N_HEADS, HEAD, HEAD), F32)],
        compiler_params=_params("arbitrary"),
        name="k2_rwkv_chunked",
    )(at, bt, kt, rt, v, bg, kg, gam)


def _tail(h, p_ref, nffn_ref, up_ref, down_ref, nple_ref, gate_ref, proj_ref):
    hn = _rms(h, nffn_ref[...]).astype(BF16)
    mlp = None
    for c in range(D_FF // FF_CHUNK):
        a = jnp.maximum(_dot(hn, up_ref[:, c * FF_CHUNK:(c + 1) * FF_CHUNK]), 0.0)
        part = _dot((a * a).astype(BF16), down_ref[c * FF_CHUNK:(c + 1) * FF_CHUNK, :])
        mlp = part if mlp is None else mlp + part
    h = h + mlp
    gate = _sigmoid(_dot(_rms(h, nple_ref[...]).astype(BF16), gate_ref[...]))
    return h + _dot(p_ref[...].astype(BF16), proj_ref[...]) * gate


def _k3_kernel(y_ref, g_ref, bv_ref, yb_ref, x_ref, p_ref, gnmean_ref, lnw_ref, lnb_ref, wout_ref,
               nffn_ref, up_ref, down_ref, nple_ref, gate_ref, proj_ref, nnext_ref, h_ref, hn_ref, perm_ref):
    y = y_ref[...]
    d = y - _dot(y.astype(BF16), gnmean_ref[...])
    var = _dot((d * d).astype(BF16), gnmean_ref[...])
    yn = d * lax.rsqrt(var + GN_EPS) * lnw_ref[...] + lnb_ref[...]
    ya = ((yn + bv_ref[...]) * g_ref[...]).astype(BF16)
    h = x_ref[...] + _dot(ya, wout_ref[0:D_RWKV, :]) + _dot(yb_ref[...], wout_ref[D_RWKV:, :])
    h = _tail(h, p_ref, nffn_ref, up_ref, down_ref, nple_ref, gate_ref, proj_ref)
    h_ref[...] = h
    hn = _rms(h, nnext_ref[...])
    blocks = hn.shape[0] // S5_BLOCK
    for c, lanes in enumerate(_lane_chunks(D_MODEL)):
        perm_ref[c] = hn[:, lanes]
    for s in range(S5_BLOCK):
        for c, lanes in enumerate(_lane_chunks(D_MODEL)):
            hn_ref[s, :, lanes] = perm_ref[c, pl.ds(s, blocks, stride=S5_BLOCK), :].astype(BF16)


def _k3(y, g, bv, yb, x2d, p3d, ln_w, ln_b, wout, nffn, up, down, nple, gate, proj, nnext):
    t = x2d.shape[0]
    rows = min(TAIL_ROWS, t)
    consts = [_head_ones(1.0 / HEAD), ln_w, ln_b, wout, nffn, up, down, nple, gate, proj, nnext]
    return pl.pallas_call(
        _k3_kernel,
        grid=(t // rows,),
        in_specs=[_row_spec(rows, D_RWKV), _row_spec(rows, D_RWKV), _row_spec(rows, D_RWKV),
                  _row_spec(rows, D_CONV), _row_spec(rows, D_MODEL), _layer_spec(rows, 0)]
                 + [_const_spec(c.shape) for c in consts],
        out_specs=[_row_spec(rows, D_MODEL), _posmajor_spec(rows)],
        out_shape=[jax.ShapeDtypeStruct((t, D_MODEL), F32),
                   jax.ShapeDtypeStruct((S5_BLOCK, t // S5_BLOCK, D_MODEL), BF16)],
        scratch_shapes=[pltpu.VMEM((D_MODEL // 128, rows, 128), F32)],
        compiler_params=_params("arbitrary"),
        name="k3_mix_out_mlp_ple",
    )(y, g, bv, yb, x2d, p3d, *consts)


def _shift_rows(x, s):
    n = x.shape[0]
    if s % 8 == 0:
        return jnp.concatenate([jnp.zeros((s, x.shape[1]), x.dtype), x[:n - s]], axis=0)
    keep = lax.broadcasted_iota(jnp.int32, x.shape, 0) >= s
    return jnp.where(keep, pltpu.roll(x, s, axis=0), 0.0)


def _swap_halves(x):
    width = x.shape[1]
    is_re = (lax.broadcasted_iota(jnp.int32, x.shape, 1) // SSM_STATE) % 2 == 0
    return jnp.where(is_re, pltpu.roll(x, width - SSM_STATE, axis=1), pltpu.roll(x, SSM_STATE, axis=1))


def _k4_kernel(x_ref, c0_ref, pc_ref, qc_ref, arr_ref, ais_ref, dsk_ref, y_ref, wt_ref, carry_ref):
    nb, rows, lanes = x_ref.shape
    gl = lanes // SSM_GROUP
    sw = 2 * SSM_STATE
    zero = jnp.zeros((), BF16)

    pw = 2 * SSM_GROUP
    npair = lanes // pw

    @pl.when(pl.program_id(1) == 0)
    def _():
        carry_ref[...] = jnp.zeros_like(carry_ref)
        c_row = lax.broadcasted_iota(jnp.int32, (gl * sw, lanes), 0) // sw
        c_col = lax.broadcasted_iota(jnp.int32, (gl * sw, lanes), 1) // SSM_GROUP
        cbd = jnp.where(c_row == c_col, jnp.concatenate([c0_ref[...]] * gl, axis=0), zero)
        b_row = lax.broadcasted_iota(jnp.int32, (lanes, gl * sw), 0) // SSM_GROUP
        b_col = lax.broadcasted_iota(jnp.int32, (lanes, gl * sw), 1) // sw
        for s in range(nb):
            lam_b = jnp.concatenate([pc_ref[j][s * pw:(s + 1) * pw, :] for j in range(npair)], axis=0)
            bbd = jnp.where(b_row == b_col, jnp.concatenate([lam_b] * gl, axis=1), zero)
            wt_ref[s * lanes:(s + 1) * lanes, :] = _dot(bbd, cbd).astype(BF16)

    slot = lax.broadcasted_iota(jnp.int32, (1, 128), 1) // pw

    def gather(src_of, src_slot_of, count):
        cols = []
        for w in range(count // 4):
            col = None
            for kk in range(4):
                src = src_of(4 * w + kk)
                shift = (pw * (kk - src_slot_of(4 * w + kk))) % 128
                src = pltpu.roll(src, shift, axis=1) if shift else src
                col = src if col is None else jnp.where(slot == kk, src, col)
            cols.append(col)
        return jnp.concatenate(cols, axis=1)

    p_row = (lax.broadcasted_iota(jnp.int32, (nb * pw, 2 * sw), 0) // SSM_GROUP) % 2
    p_col = lax.broadcasted_iota(jnp.int32, (nb * pw, 2 * sw), 1) // sw
    parts = []
    for j in range(npair):
        lhs = gather(lambda s: x_ref[s][:, 128 * (j // 4):128 * (j // 4 + 1)], lambda s: j % 4, nb)
        pbd = jnp.where(p_row == p_col, jnp.concatenate([pc_ref[j]] * 2, axis=1), zero)
        parts.append(_dot(lhs, pbd))
    st = jnp.concatenate(parts, axis=1)

    first = lax.broadcasted_iota(jnp.int32, st.shape, 0) == 0
    h8 = carry_ref[...]
    h_in = h8[0:1, :]
    st = st + jnp.where(first, (arr_ref[0:1, :] * h8 + ais_ref[0:1, :] * _swap_halves(h8))[0:1, :], 0.0)
    for k in range(arr_ref.shape[0]):
        if (1 << k) >= rows:
            break
        sh = _shift_rows(st, 1 << k)
        st = st + arr_ref[k:k + 1, :] * sh + ais_ref[k:k + 1, :] * _swap_halves(sh)
    carry_ref[0:1, :] = st[rows - 1:rows, :]
    hb = jnp.where(first, h_in, _shift_rows(st, 1)).astype(BF16)

    q_row = lax.broadcasted_iota(jnp.int32, (2 * sw, nb * pw), 0) // sw
    q_col = (lax.broadcasted_iota(jnp.int32, (2 * sw, nb * pw), 1) // SSM_GROUP) % 2
    y_in = []
    for j in range(npair):
        qbd = jnp.where(q_row == q_col, jnp.concatenate([qc_ref[j]] * 2, axis=0), zero)
        y_in.append(_dot(hb[:, j * 2 * sw:(j + 1) * 2 * sw], qbd).astype(BF16))
    dsk = dsk_ref[...]
    for t in range(nb):
        xcat = jnp.concatenate([x_ref[s] for s in range(t + 1)], axis=1)
        y_t = gather(lambda j: y_in[j][:, 128 * (t // 4):128 * (t // 4 + 1)], lambda j: t % 4, npair)
        y = _dot(xcat, wt_ref[(nb - 1 - t) * lanes:, :]) + y_t.astype(F32)
        y_ref[t] = _gelu_tanh(y + dsk * x_ref[t].astype(F32)).astype(BF16)


def _k4(xs, c0, pc, qc, arr, ais, dsk):
    nb, m, d = xs.shape
    lanes = 256
    rows = min(S5_ROWS, m)
    sw = 2 * SSM_STATE * (lanes // SSM_GROUP)
    return pl.pallas_call(
        _k4_kernel,
        grid=(d // lanes, m // rows),
        in_specs=[pl.BlockSpec((nb, rows, lanes), lambda q, i: (0, i, q)),
                  pl.BlockSpec((None,) + c0.shape[1:], lambda q, i: (q, 0, 0)),
                  pl.BlockSpec((None,) + pc.shape[1:], lambda q, i: (q, 0, 0, 0)),
                  pl.BlockSpec((None,) + qc.shape[1:], lambda q, i: (q, 0, 0, 0)),
                  pl.BlockSpec((None,) + arr.shape[1:], lambda q, i: (q, 0, 0)),
                  pl.BlockSpec((None,) + ais.shape[1:], lambda q, i: (q, 0, 0)),
                  pl.BlockSpec((1, lanes), lambda q, i: (0, q))],
        out_specs=pl.BlockSpec((nb, rows, lanes), lambda q, i: (0, i, q)),
        out_shape=jax.ShapeDtypeStruct((nb, m, d), BF16),
        scratch_shapes=[pltpu.VMEM((nb * lanes, lanes), BF16), pltpu.VMEM((8, sw), F32)],
        compiler_params=_params("arbitrary", "arbitrary"),
        name="k4_s5",
    )(xs, c0, pc, qc, arr, ais, dsk)


def _s5_tables(lam_re, lam_im, log_step, b_re, b_im, c_re, c_im, d_skip, n_blocks):
    g, p, c, nb = SSM_GROUPS, SSM_STATE, SSM_GROUP, S5_BLOCK
    lre = jnp.minimum(lam_re.astype(F32), -1e-4)
    lim = lam_im.astype(F32)
    step = jnp.exp(log_step.astype(F32))[:, None]
    ar, ai = lre * step, lim * step
    n = jnp.arange(nb + 1, dtype=F32)[:, None, None]
    mag = jnp.exp(n * ar)
    pr, pi = mag * jnp.cos(n * ai), mag * jnp.sin(n * ai)
    nr, ni = pr[1] - 1.0, pi[1]
    den = lre * lre + lim * lim
    qr, qi = (nr * lre + ni * lim) / den, (ni * lre - nr * lim) / den
    bre, bim = b_re.astype(F32), b_im.astype(F32)
    bbr = qr[..., None] * bre - qi[..., None] * bim
    bbi = qr[..., None] * bim + qi[..., None] * bre
    cre, cim = c_re.astype(F32), c_im.astype(F32)
    tiles, pairs = g * c // 256, 256 // (2 * c)
    split = lambda m: m.reshape((tiles, pairs, 2) + m.shape[1:])
    bt_r, bt_i = bbr.transpose(0, 2, 1), bbi.transpose(0, 2, 1)
    b_for_re = split(jnp.stack([bt_r, bt_i], axis=2))[:, :, None]
    b_for_im = split(jnp.stack([-bt_i, bt_r], axis=2))[:, :, None]
    ps_r = split(pr[nb - 1::-1].transpose(1, 0, 2)).transpose(0, 1, 3, 2, 4)[..., None, None, :]
    ps_i = split(pi[nb - 1::-1].transpose(1, 0, 2)).transpose(0, 1, 3, 2, 4)[..., None, None, :]
    pc = (ps_r * b_for_re + ps_i * b_for_im).reshape(tiles, pairs, nb * 2 * c, 2 * p)
    ct_r, ct_i = cre.transpose(0, 2, 1), cim.transpose(0, 2, 1)
    c0 = jnp.stack([ct_r, -ct_i], axis=1).reshape(tiles, 256 // c, 2 * p, c)
    c0 = c0.transpose(0, 2, 1, 3).reshape(tiles, 2 * p, 256)
    c_for_re = split(jnp.stack([ct_r, -ct_i], axis=1)).transpose(0, 1, 3, 4, 2, 5)[:, :, :, :, None]
    c_for_im = split(jnp.stack([-ct_i, -ct_r], axis=1)).transpose(0, 1, 3, 4, 2, 5)[:, :, :, :, None]
    pt_r = split(pr[1:].transpose(1, 2, 0)).transpose(0, 1, 3, 4, 2)[:, :, None, :, :, :, None]
    pt_i = split(pi[1:].transpose(1, 2, 0)).transpose(0, 1, 3, 4, 2)[:, :, None, :, :, :, None]
    qc = (pt_r * c_for_re + pt_i * c_for_im).reshape(tiles, pairs, 2 * p, nb * 2 * c)
    levels = max(1, int(np.ceil(np.log2(max(n_blocks, 2)))))
    a_re, a_im = [pr[nb]], [pi[nb]]
    for _ in range(levels - 1):
        r, i = a_re[-1], a_im[-1]
        a_re.append(r * r - i * i)
        a_im.append(2.0 * r * i)
    a_re, a_im = jnp.stack(a_re, axis=0), jnp.stack(a_im, axis=0)
    arr = jnp.stack([a_re, a_re], axis=2).reshape(levels, tiles, -1).transpose(1, 0, 2)
    ais = jnp.stack([-a_im, a_im], axis=2).reshape(levels, tiles, -1).transpose(1, 0, 2)
    return c0.astype(BF16), pc.astype(BF16), qc.astype(BF16), arr, ais, d_skip.astype(F32).reshape(1, g * c)


def _k5_kernel(ys_ref, h_ref, p_ref, w1_ref, w2_ref, nffn_ref, up_ref, down_ref, nple_ref, gate_ref,
               proj_ref, nfinal_ref, o_ref, perm_ref):
    blocks = ys_ref.shape[1]
    for s in range(S5_BLOCK):
        for c, lanes in enumerate(_lane_chunks(D_MODEL)):
            perm_ref[c, pl.ds(s, blocks, stride=S5_BLOCK), :] = ys_ref[s, :, lanes].astype(F32)
    yg = jnp.concatenate([perm_ref[c] for c in range(D_MODEL // 128)], axis=1).astype(BF16)
    h = h_ref[...] + _dot(yg, w1_ref[...]) * _sigmoid(_dot(yg, w2_ref[...]))
    h = _tail(h, p_ref, nffn_ref, up_ref, down_ref, nple_ref, gate_ref, proj_ref)
    o_ref[...] = _rms(h, nfinal_ref[...])


def _k5(ys, h, p3d, w1, w2, nffn, up, down, nple, gate, proj, nfinal):
    t = h.shape[0]
    rows = min(TAIL_ROWS, t)
    consts = [w1, w2, nffn, up, down, nple, gate, proj, nfinal]
    return pl.pallas_call(
        _k5_kernel,
        grid=(t // rows,),
        in_specs=[_posmajor_spec(rows), _row_spec(rows, D_MODEL), _layer_spec(rows, 1)]
                 + [_const_spec(c.shape) for c in consts],
        out_specs=_row_spec(rows, D_MODEL),
        out_shape=jax.ShapeDtypeStruct((t, D_MODEL), F32),
        scratch_shapes=[pltpu.VMEM((D_MODEL // 128, rows, 128), F32)],
        compiler_params=_params("arbitrary"),
        name="k5_glu_mlp_ple_norm",
    )(ys, h, p3d, *consts)


def _row(vec):
    return vec.astype(F32).reshape(1, -1)


def _lora_rows(m, first, width):
    return jnp.zeros((D_LORA, width), F32).at[first:first + m.shape[0]].set(m.astype(F32)).astype(BF16)


def kernel(x, p, l0_norm_mix, l0_w_in, l0_shift_mu, l0_w0, l0_w_lora_up, l0_a0, l0_a_lora_up, l0_g_lora_up, l0_k_k, l0_k_a, l0_r_k, l0_ln_w, l0_ln_b, l0_conv_w, l0_w_out, l0_norm_ffn, l0_ffn_up, l0_ffn_down, l0_norm_ple, l0_ple_gate, l0_ple_proj, l1_norm_mix, l1_lambda_re, l1_lambda_im, l1_log_step, l1_b_re, l1_b_im, l1_c_re, l1_c_im, l1_d_skip, l1_glu_w1, l1_glu_w2, l1_norm_ffn, l1_ffn_up, l1_ffn_down, l1_norm_ple, l1_ple_gate, l1_ple_proj, norm_final):
    bsz, t, _ = x.shape
    assert bsz == 1 and t % K1_ROWS == 0 and t % TAIL_ROWS == 0
    x2d = x.reshape(t, D_MODEL)
    p3d = p.reshape(p.shape[0], t, D_PLE)

    at, bt, kt, rt, v, bg, kg, gam, g, bv, yb = _k1(
        x2d, _row(l0_norm_mix), l0_w_in.astype(BF16), _row(l0_shift_mu[:D_IN_A]), _row(l0_w0),
        _lora_rows(l0_w_lora_up, 0, D_RWKV), _row(l0_a0), _lora_rows(l0_a_lora_up, LORA_W, D_RWKV),
        _lora_rows(l0_g_lora_up, LORA_W + LORA_A, D_RWKV), _row(l0_k_k), _row(l0_k_a), _row(l0_r_k),
        l0_conv_w.astype(F32))
    y = _k2(at, bt, kt, rt, v, bg, kg, gam)
    h, hn = _k3(y, g, bv, yb, x2d, p3d, _row(l0_ln_w), _row(l0_ln_b), l0_w_out.astype(BF16),
                _row(l0_norm_ffn), l0_ffn_up.astype(BF16), l0_ffn_down.astype(BF16),
                _row(l0_norm_ple), l0_ple_gate.astype(BF16), l0_ple_proj.astype(BF16), _row(l1_norm_mix))

    tables = _s5_tables(l1_lambda_re, l1_lambda_im, l1_log_step, l1_b_re, l1_b_im, l1_c_re, l1_c_im,
                        l1_d_skip, t // S5_BLOCK)
    ys = _k4(hn, *tables)

    out = _k5(ys, h, p3d, l1_glu_w1.astype(BF16), l1_glu_w2.astype(BF16), _row(l1_norm_ffn),
              l1_ffn_up.astype(BF16), l1_ffn_down.astype(BF16), _row(l1_norm_ple),
              l1_ple_gate.astype(BF16), l1_ple_proj.astype(BF16), _row(norm_final))
    return out.reshape(bsz, t, D_MODEL)
```

```python
import numpy as np
import jax
import jax.numpy as jnp
from jax import lax
from jax.experimental import pallas as pl
from jax.experimental.pallas import tpu as pltpu

F32 = jnp.float32
BF16 = jnp.bfloat16

D_MODEL = 1024
N_HEADS = 8
HEAD = 64
D_RWKV = N_HEADS * HEAD
D_CONV = D_MODEL - D_RWKV
LORA_W, LORA_A, LORA_G = 64, 64, 128
D_LORA = LORA_W + LORA_A + LORA_G
D_IN_A = 3 * D_RWKV + D_LORA
D_IN = D_IN_A + 3 * D_CONV
GN_EPS = 64e-5
RMS_EPS = 1e-6
SSM_GROUPS, SSM_GROUP, SSM_STATE = 64, 16, 64
D_FF = 4 * D_MODEL
D_PLE = 256

VMEM_LIMIT = 56 * 1024 * 1024
HIGHEST = lax.Precision.HIGHEST

RWKV_CHUNK = 64
K1_ROWS = 256
K2_CHUNKS = 4
TAIL_ROWS = 512
FF_CHUNK = 1024
S5_BLOCK = 16
S5_ROWS = 512


def _dot(a, b):
    return jnp.dot(a, b, preferred_element_type=F32)


def _dot_nt(a, b):
    return lax.dot_general(a, b, (((1,), (1,)), ((), ())), preferred_element_type=F32)


def _dot_tn(a, b):
    return lax.dot_general(a, b, (((0,), (0,)), ((), ())), preferred_element_type=F32)


def _split_bf16(x, n):
    pieces = []
    for _ in range(n - 1):
        p = x.astype(BF16)
        pieces.append(p)
        x = x - p.astype(F32)
    pieces.append(x.astype(BF16))
    return pieces


def _dot_wide_lhs(x, w, n):
    return sum(_dot(p, w) for p in _split_bf16(x, n))


def _dot_wide_rhs(w, x, n):
    return sum(_dot(w, p) for p in _split_bf16(x, n))


def _rms(x, gain):
    return x * lax.rsqrt(jnp.mean(x * x, axis=-1, keepdims=True) + RMS_EPS) * gain


def _sigmoid(x):
    return 1.0 / (1.0 + jnp.exp(-x))


def _softplus(x):
    return jnp.maximum(x, 0.0) + jnp.log(1.0 + jnp.exp(-jnp.abs(x)))


def _gelu_tanh(x):
    return 0.5 * x * (1.0 + jnp.tanh(np.sqrt(2.0 / np.pi).astype(np.float32) * (x + 0.044715 * (x * x * x))))


def _const_spec(shape):
    nd = len(shape)
    return pl.BlockSpec(shape, lambda *_: (0,) * nd, pipeline_mode=pl.Buffered(1))


def _row_spec(rows, cols):
    return pl.BlockSpec((rows, cols), lambda i: (i, 0))


def _layer_spec(rows, layer):
    return pl.BlockSpec((None, rows, D_PLE), lambda i: (layer, i, 0))


def _posmajor_spec(rows):
    return pl.BlockSpec((S5_BLOCK, rows // S5_BLOCK, D_MODEL), lambda i: (0, i, 0))


def _lane_chunks(width):
    return [slice(c, c + 128) for c in range(0, width, 128)]


def _params(*sem):
    return pltpu.CompilerParams(dimension_semantics=sem, vmem_limit_bytes=VMEM_LIMIT)


def _head_ones(scale):
    lane = np.arange(D_RWKV)
    return jnp.asarray(scale * ((lane[:, None] // HEAD) == (lane[None, :] // HEAD)), BF16)


def _k1_kernel(x_ref, gmix_ref, win_ref, mu_ref, w0_ref, ww_ref, a0_ref, wa_ref, wg_ref, kk_ref, ka_ref,
               rk_ref, cw_ref, tri_ref, sel_ref, hsum_ref,
               at_ref, bt_ref, kt_ref, rt_ref, v_ref, bg_ref, kg_ref, gam_ref, g_ref, bv_ref, yb_ref,
               zs_ref, us_ref):
    rows = x_ref.shape[0]

    @pl.when(pl.program_id(0) == 0)
    def _():
        zs_ref[0:8, :] = jnp.zeros((8, D_IN_A), F32)
        us_ref[0:8, :] = jnp.zeros((8, D_CONV), F32)

    xn = _rms(x_ref[...], gmix_ref[...])
    z = _dot(xn.astype(BF16), win_ref[...])

    za = z[:, :D_IN_A]
    zs_ref[8:8 + rows, :] = za
    za_prev = zs_ref[7:7 + rows, :]
    zs_ref[7:8, :] = za[rows - 1:rows, :]
    za = za + mu_ref[...] * (za_prev - za)

    r = za[:, 0:D_RWKV]
    k = za[:, D_RWKV:2 * D_RWKV]
    v = za[:, 2 * D_RWKV:3 * D_RWKV]
    lora_in = za[:, 3 * D_RWKV:]

    w_log = -_softplus(-(w0_ref[...] + _dot(jnp.tanh(lora_in).astype(BF16), ww_ref[...]))) - 0.5
    logw = -jnp.exp(w_log)
    lr = _sigmoid(a0_ref[...] + _dot(lora_in.astype(BF16), wa_ref[...]))
    g_ref[...] = _dot(_sigmoid(lora_in).astype(BF16), wg_ref[...])

    lw = _split_bf16(logw, 3)
    c_incl = sum(_dot(tri_ref[...], p) for p in lw[:2])
    c_chunk = sum(_dot(sel_ref[...], p) for p in lw)
    gam_ref[...] = jnp.exp(c_chunk)
    chunk_id = lax.broadcasted_iota(jnp.int32, (rows, 1), 0) // RWKV_CHUNK
    c_tot = c_chunk[0:1, :]
    for i in range(1, rows // RWKV_CHUNK):
        c_tot = jnp.where(chunk_id == i, c_chunk[i:i + 1, :], c_tot)
    e_neg = jnp.exp(-c_incl)
    e_rest = jnp.exp(c_tot - c_incl)

    kk = k * kk_ref[...]
    kk = kk / jnp.maximum(jnp.sqrt(_dot((kk * kk).astype(BF16), hsum_ref[...])), 1e-12)
    kmod = k * (1.0 + (lr - 1.0) * ka_ref[...])
    at_ref[...] = (-kk * jnp.exp(c_incl - logw)).astype(BF16)
    bt_ref[...] = (kk * lr * e_neg).astype(BF16)
    bg_ref[...] = (kk * lr * e_rest).astype(BF16)
    kt_ref[...] = (kmod * e_neg).astype(BF16)
    kg_ref[...] = (kmod * e_rest).astype(BF16)
    rt_ref[...] = (r * jnp.exp(c_incl)).astype(BF16)
    v_ref[...] = v.astype(BF16)
    bv_ref[...] = _dot((r * kmod * rk_ref[...]).astype(BF16), hsum_ref[...]) * v

    zb = z[:, D_IN_A:]
    b_gate = zb[:, :D_CONV]
    u = zb[:, D_CONV:2 * D_CONV] * zb[:, 2 * D_CONV:]
    us_ref[8:8 + rows, :] = u
    u1 = us_ref[7:7 + rows, :]
    u2 = us_ref[6:6 + rows, :]
    us_ref[6:8, :] = u[rows - 2:rows, :]
    cw = cw_ref[...]
    yb_ref[...] = (b_gate * (cw[0:1] * u + cw[1:2] * u1 + cw[2:3] * u2)).astype(BF16)


def _k1(x2d, gmix, win, mu, w0, ww, a0, wa, wg, k_k, k_a, r_k, conv_w):
    t = x2d.shape[0]
    rows = K1_ROWS
    nchunk = rows // RWKV_CHUNK
    r_idx = np.arange(rows)
    same = (r_idx[:, None] // RWKV_CHUNK) == (r_idx[None, :] // RWKV_CHUNK)
    tri = jnp.asarray(same & (r_idx[None, :] <= r_idx[:, None]), BF16)
    sel = jnp.asarray(np.arange(8)[:, None] == (r_idx[None, :] // RWKV_CHUNK), BF16)

    consts = [gmix, win, mu, w0, ww, a0, wa, wg, k_k, k_a, r_k, conv_w, tri, sel, _head_ones(1.0)]
    wide = jax.ShapeDtypeStruct((t, D_RWKV), BF16)
    out_shape = [wide] * 7 + [
        jax.ShapeDtypeStruct((t // rows, 8, D_RWKV), F32),
        jax.ShapeDtypeStruct((t, D_RWKV), F32),
        jax.ShapeDtypeStruct((t, D_RWKV), F32),
        jax.ShapeDtypeStruct((t, D_CONV), BF16),
    ]
    out_specs = [_row_spec(rows, D_RWKV)] * 7 + [
        pl.BlockSpec((None, 8, D_RWKV), lambda i: (i, 0, 0)),
        _row_spec(rows, D_RWKV), _row_spec(rows, D_RWKV), _row_spec(rows, D_CONV)]
    return pl.pallas_call(
        _k1_kernel,
        grid=(t // rows,),
        in_specs=[_row_spec(rows, D_MODEL)] + [_const_spec(c.shape) for c in consts],
        out_specs=out_specs,
        out_shape=out_shape,
        scratch_shapes=[pltpu.VMEM((rows + 8, D_IN_A), F32), pltpu.VMEM((rows + 8, D_CONV), F32)],
        compiler_params=_params("arbitrary"),
        name="k1_inproj_prep",
    )(x2d, *consts)


def _k2_kernel(at_ref, bt_ref, kt_ref, rt_ref, v_ref, bg_ref, kg_ref, gam_ref, y_ref, s_ref):
    n = RWKV_CHUNK
    step = pl.program_id(0)

    @pl.when(step == 0)
    def _():
        s_ref[...] = jnp.zeros_like(s_ref)

    row = lax.broadcasted_iota(jnp.int32, (n, n), 0)
    col = lax.broadcasted_iota(jnp.int32, (n, n), 1)
    strict, incl, diag = col < row, col <= row, col == row
    eye = jnp.where(diag, 1.0, 0.0).astype(F32)
    sub = lax.broadcasted_iota(jnp.int32, (8, D_RWKV), 0)
    first = (step * K2_CHUNKS) % (K1_ROWS // n)

    units = [(c, h) for c in range(K2_CHUNKS) for h in range(N_HEADS)]

    def tile(ref, c, h):
        return ref[c * n:(c + 1) * n, h * HEAD:(h + 1) * HEAD]

    row2 = lax.broadcasted_iota(jnp.int32, (2 * n, 2 * n), 0)
    col2 = lax.broadcasted_iota(jnp.int32, (2 * n, 2 * n), 1) % n
    tri2 = col2 <= jnp.where(row2 < n, row2 - 1, row2 - n)
    zeros = jnp.zeros((n, HEAD), BF16)

    a = {u: tile(at_ref, *u) for u in units}
    r = {u: tile(rt_ref, *u) for u in units}
    v = {u: tile(v_ref, *u) for u in units}
    gm = {}
    for u in units:
        ar = jnp.concatenate([a[u], r[u]], axis=0)
        bk = jnp.concatenate([tile(bt_ref, *u), tile(kt_ref, *u)], axis=0)
        gm[u] = jnp.where(tri2, _dot_nt(ar, bk), 0.0)
    gmb = {u: gm[u].astype(BF16) for u in units}
    wv = {u: _dot(gmb[u], jnp.concatenate([zeros, v[u]], axis=0)) for u in units}

    q = {u: gm[u][:n, :n] for u in units}
    tinv = {u: eye + q[u] for u in units}
    for u in units:
        qb = q[u].astype(BF16)
        q[u] = _dot(qb, qb)
    for level in range(1, 6):
        for u in units:
            qb = q[u].astype(BF16)
            if level < 5:
                tq = _dot(jnp.concatenate([tinv[u].astype(BF16), qb], axis=0), qb)
                tinv[u], q[u] = tinv[u] + tq[:n], tq[n:]
            else:
                tinv[u] = tinv[u] + _dot(tinv[u].astype(BF16), qb)

    au, ry, mc = {}, {}, {}
    for u in units:
        rhs = jnp.concatenate([a[u], wv[u][:n].astype(BF16)], axis=1)
        au[u] = _dot(tinv[u].astype(BF16), rhs).astype(BF16)
    for u in units:
        rhs = jnp.concatenate([au[u], jnp.zeros_like(au[u])], axis=0)
        ry[u] = jnp.concatenate([r[u].astype(F32), wv[u][n:]], axis=1) + _dot(gmb[u][n:], rhs)
    for u in units:
        lhs = jnp.concatenate([tile(bg_ref, *u), tile(kg_ref, *u)], axis=0)
        rhs = jnp.concatenate([au[u], jnp.concatenate([zeros, v[u]], axis=1)], axis=0)
        mc[u] = _dot_tn(lhs, rhs)
    for u in units:
        c, h = u
        gam_row = jnp.sum(jnp.where(sub == first + c, gam_ref[...], 0.0), axis=0, keepdims=True)
        m_t = jnp.where(diag, gam_row[:, h * HEAD:(h + 1) * HEAD], 0.0) + mc[u][:, :HEAD]
        mr = _dot(jnp.concatenate([m_t, ry[u][:, :HEAD]], axis=0).astype(BF16), s_ref[h].astype(BF16))
        s_ref[h] = mr[:HEAD] + mc[u][:, HEAD:]
        y_ref[c * n:(c + 1) * n, h * HEAD:(h + 1) * HEAD] = mr[HEAD:] + ry[u][:, HEAD:]


def _k2(at, bt, kt, rt, v, bg, kg, gam):
    t = at.shape[0]
    rows = RWKV_CHUNK * K2_CHUNKS
    per = K1_ROWS // rows
    seq = _row_spec(rows, D_RWKV)
    return pl.pallas_call(
        _k2_kernel,
        grid=(t // rows,),
        in_specs=[seq] * 7 + [pl.BlockSpec((None, 8, D_RWKV), lambda i: (i // per, 0, 0))],
        out_specs=seq,
        out_shape=jax.ShapeDtypeStruct((t, D_RWKV), F32),
        scratch_shapes=[pltpu.VMEM((N_HEADS, HEAD, HEAD), F32)],
        compiler_params=_params("arbitrary"),
        name="k2_rwkv_chunked",
    )(at, bt, kt, rt, v, bg, kg, gam)


def _tail(h, p_ref, nffn_ref, up_ref, down_ref, nple_ref, gate_ref, proj_ref):
    hn = _rms(h, nffn_ref[...]).astype(BF16)
    mlp = None
    for c in range(D_FF // FF_CHUNK):
        a = jnp.maximum(_dot(hn, up_ref[:, c * FF_CHUNK:(c + 1) * FF_CHUNK]), 0.0)
        part = _dot((a * a).astype(BF16), down_ref[c * FF_CHUNK:(c + 1) * FF_CHUNK, :])
        mlp = part if mlp is None else mlp + part
    h = h + mlp
    gate = _sigmoid(_dot(_rms(h, nple_ref[...]).astype(BF16), gate_ref[...]))
    return h + _dot(p_ref[...].astype(BF16), proj_ref[...]) * gate


def _k3_kernel(y_ref, g_ref, bv_ref, yb_ref, x_ref, p_ref, gnmean_ref, lnw_ref, lnb_ref, wout_ref,
               nffn_ref, up_ref, down_ref, nple_ref, gate_ref, proj_ref, nnext_ref, h_ref, hn_ref, perm_ref):
    y = y_ref[...]
    d = y - _dot(y.astype(BF16), gnmean_ref[...])
    var = _dot((d * d).astype(BF16), gnmean_ref[...])
    yn = d * lax.rsqrt(var + GN_EPS) * lnw_ref[...] + lnb_ref[...]
    ya = ((yn + bv_ref[...]) * g_ref[...]).astype(BF16)
    h = x_ref[...] + _dot(ya, wout_ref[0:D_RWKV, :]) + _dot(yb_ref[...], wout_ref[D_RWKV:, :])
    h = _tail(h, p_ref, nffn_ref, up_ref, down_ref, nple_ref, gate_ref, proj_ref)
    h_ref[...] = h
    hn = _rms(h, nnext_ref[...])
    blocks = hn.shape[0] // S5_BLOCK
    for c, lanes in enumerate(_lane_chunks(D_MODEL)):
        perm_ref[c] = hn[:, lanes]
    for s in range(S5_BLOCK):
        for c, lanes in enumerate(_lane_chunks(D_MODEL)):
            hn_ref[s, :, lanes] = perm_ref[c, pl.ds(s, blocks, stride=S5_BLOCK), :].astype(BF16)


def _k3(y, g, bv, yb, x2d, p3d, ln_w, ln_b, wout, nffn, up, down, nple, gate, proj, nnext):
    t = x2d.shape[0]
    rows = min(TAIL_ROWS, t)
    consts = [_head_ones(1.0 / HEAD), ln_w, ln_b, wout, nffn, up, down, nple, gate, proj, nnext]
    return pl.pallas_call(
        _k3_kernel,
        grid=(t // rows,),
        in_specs=[_row_spec(rows, D_RWKV), _row_spec(rows, D_RWKV), _row_spec(rows, D_RWKV),
                  _row_spec(rows, D_CONV), _row_spec(rows, D_MODEL), _layer_spec(rows, 0)]
                 + [_const_spec(c.shape) for c in consts],
        out_specs=[_row_spec(rows, D_MODEL), _posmajor_spec(rows)],
        out_shape=[jax.ShapeDtypeStruct((t, D_MODEL), F32),
                   jax.ShapeDtypeStruct((S5_BLOCK, t // S5_BLOCK, D_MODEL), BF16)],
        scratch_shapes=[pltpu.VMEM((D_MODEL // 128, rows, 128), F32)],
        compiler_params=_params("arbitrary"),
        name="k3_mix_out_mlp_ple",
    )(y, g, bv, yb, x2d, p3d, *consts)


def _shift_rows(x, s):
    n = x.shape[0]
    if s % 8 == 0:
        return jnp.concatenate([jnp.zeros((s, x.shape[1]), x.dtype), x[:n - s]], axis=0)
    keep = lax.broadcasted_iota(jnp.int32, x.shape, 0) >= s
    return jnp.where(keep, pltpu.roll(x, s, axis=0), 0.0)


def _k4_kernel(x_ref, c0_ref, pc_ref, qc_ref, are_ref, aim_ref, dsk_ref, y_ref, wt_ref, carry_ref):
    nb, rows, lanes = x_ref.shape
    gl = lanes // SSM_GROUP
    sw = 2 * SSM_STATE
    zero = jnp.zeros((), BF16)

    pw = 2 * SSM_GROUP
    npair = lanes // pw

    @pl.when(pl.program_id(1) == 0)
    def _():
        carry_ref[...] = jnp.zeros_like(carry_ref)
        c_row = lax.broadcasted_iota(jnp.int32, (gl * sw, lanes), 0) // sw
        c_col = lax.broadcasted_iota(jnp.int32, (gl * sw, lanes), 1) // SSM_GROUP
        cbd = jnp.where(c_row == c_col, jnp.concatenate([c0_ref[...]] * gl, axis=0), zero)
        b_row = lax.broadcasted_iota(jnp.int32, (lanes, gl * sw), 0) // SSM_GROUP
        b_col = lax.broadcasted_iota(jnp.int32, (lanes, gl * sw), 1) // sw
        for s in range(nb):
            lam_b = jnp.concatenate([pc_ref[j][s * pw:(s + 1) * pw, :] for j in range(npair)], axis=0)
            bbd = jnp.where(b_row == b_col, jnp.concatenate([lam_b] * gl, axis=1), zero)
            wt_ref[s * lanes:(s + 1) * lanes, :] = _dot(bbd, cbd).astype(BF16)

    slot = lax.broadcasted_iota(jnp.int32, (1, 128), 1) // pw

    def gather(src_of, src_slot_of, count):
        cols = []
        for w in range(count // 4):
            col = None
            for kk in range(4):
                src = src_of(4 * w + kk)
                shift = (pw * (kk - src_slot_of(4 * w + kk))) % 128
                src = pltpu.roll(src, shift, axis=1) if shift else src
                col = src if col is None else jnp.where(slot == kk, src, col)
            cols.append(col)
        return jnp.concatenate(cols, axis=1)

    ns = SSM_STATE
    p_same = ((lax.broadcasted_iota(jnp.int32, (nb * pw, 2 * ns), 0) // SSM_GROUP) % 2
              == lax.broadcasted_iota(jnp.int32, (nb * pw, 2 * ns), 1) // ns)
    parts_r, parts_i = [], []
    for j in range(npair):
        lhs = gather(lambda s: x_ref[s][:, 128 * (j // 4):128 * (j // 4 + 1)], lambda s: j % 4, nb)
        pcj = pc_ref[j]
        parts_r.append(_dot(lhs, jnp.where(p_same, jnp.concatenate([pcj[:, :ns]] * 2, axis=1), zero)))
        parts_i.append(_dot(lhs, jnp.where(p_same, jnp.concatenate([pcj[:, ns:]] * 2, axis=1), zero)))
    sr = jnp.concatenate(parts_r, axis=1)
    si = jnp.concatenate(parts_i, axis=1)

    first = lax.broadcasted_iota(jnp.int32, sr.shape, 0) == 0
    hr_in, hi_in = carry_ref[0:1, :], carry_ref[8:9, :]
    ar, ai = are_ref[0:1, :], aim_ref[0:1, :]
    sr = sr + jnp.where(first, ar * hr_in - ai * hi_in, 0.0)
    si = si + jnp.where(first, ar * hi_in + ai * hr_in, 0.0)
    for k in range(are_ref.shape[0]):
        if (1 << k) >= rows:
            break
        ar, ai = are_ref[k:k + 1, :], aim_ref[k:k + 1, :]
        tr, ti = _shift_rows(sr, 1 << k), _shift_rows(si, 1 << k)
        sr, si = sr + ar * tr - ai * ti, si + ar * ti + ai * tr
    carry_ref[0:1, :] = sr[rows - 1:rows, :]
    carry_ref[8:9, :] = si[rows - 1:rows, :]
    hbr = jnp.where(first, hr_in, _shift_rows(sr, 1)).astype(BF16)
    hbi = jnp.where(first, hi_in, _shift_rows(si, 1)).astype(BF16)

    y_in = []
    for j in range(npair):
        qcj = qc_ref[j]
        q_re = jnp.where(p_same, jnp.concatenate([qcj[:, :ns]] * 2, axis=1), zero)
        q_im = jnp.where(p_same, jnp.concatenate([qcj[:, ns:]] * 2, axis=1), zero)
        cols = slice(j * 2 * ns, (j + 1) * 2 * ns)
        y_in.append((_dot_nt(hbr[:, cols], q_re) + _dot_nt(hbi[:, cols], q_im)).astype(BF16))
    dsk = dsk_ref[...]
    for t in range(nb):
        xcat = jnp.concatenate([x_ref[s] for s in range(t + 1)], axis=1)
        y_t = gather(lambda j: y_in[j][:, 128 * (t // 4):128 * (t // 4 + 1)], lambda j: t % 4, npair)
        y = _dot(xcat, wt_ref[(nb - 1 - t) * lanes:, :]) + y_t.astype(F32)
        y_ref[t] = _gelu_tanh(y + dsk * x_ref[t].astype(F32)).astype(BF16)


def _k4(xs, c0, pc, qc, a_re, a_im, dsk):
    nb, m, d = xs.shape
    lanes = 256
    rows = min(S5_ROWS, m)
    return pl.pallas_call(
        _k4_kernel,
        grid=(d // lanes, m // rows),
        in_specs=[pl.BlockSpec((nb, rows, lanes), lambda q, i: (0, i, q)),
                  pl.BlockSpec((None,) + c0.shape[1:], lambda q, i: (q, 0, 0)),
                  pl.BlockSpec((None,) + pc.shape[1:], lambda q, i: (q, 0, 0, 0)),
                  pl.BlockSpec((None,) + qc.shape[1:], lambda q, i: (q, 0, 0, 0)),
                  pl.BlockSpec((None,) + a_re.shape[1:], lambda q, i: (q, 0, 0)),
                  pl.BlockSpec((None,) + a_im.shape[1:], lambda q, i: (q, 0, 0)),
                  pl.BlockSpec((1, lanes), lambda q, i: (0, q))],
        out_specs=pl.BlockSpec((nb, rows, lanes), lambda q, i: (0, i, q)),
        out_shape=jax.ShapeDtypeStruct((nb, m, d), BF16),
        scratch_shapes=[pltpu.VMEM((nb * lanes, lanes), BF16),
                        pltpu.VMEM((16, SSM_STATE * (lanes // SSM_GROUP)), F32)],
        compiler_params=_params("arbitrary", "arbitrary"),
        name="k4_s5",
    )(xs, c0, pc, qc, a_re, a_im, dsk)


def _s5_tables(lam_re, lam_im, log_step, b_re, b_im, c_re, c_im, d_skip, n_blocks):
    g, p, c, nb = SSM_GROUPS, SSM_STATE, SSM_GROUP, S5_BLOCK
    lre = jnp.minimum(lam_re.astype(F32), -1e-4)
    lim = lam_im.astype(F32)
    step = jnp.exp(log_step.astype(F32))[:, None]
    ar, ai = lre * step, lim * step
    n = jnp.arange(nb + 1, dtype=F32)[:, None, None]
    mag = jnp.exp(n * ar)
    pr, pi = mag * jnp.cos(n * ai), mag * jnp.sin(n * ai)
    nr, ni = pr[1] - 1.0, pi[1]
    den = lre * lre + lim * lim
    qr, qi = (nr * lre + ni * lim) / den, (ni * lre - nr * lim) / den
    bre, bim = b_re.astype(F32), b_im.astype(F32)
    bbr = qr[..., None] * bre - qi[..., None] * bim
    bbi = qr[..., None] * bim + qi[..., None] * bre
    cre, cim = c_re.astype(F32), c_im.astype(F32)
    tiles, pairs = g * c // 256, 256 // (2 * c)
    split = lambda m: m.reshape((tiles, pairs, 2) + m.shape[1:])
    both = lambda a, b: jnp.concatenate([a, b], axis=-1)
    bt_r, bt_i = bbr.transpose(0, 2, 1), bbi.transpose(0, 2, 1)
    b_for_re = split(both(bt_r, bt_i))[:, :, None]
    b_for_im = split(both(-bt_i, bt_r))[:, :, None]
    ps_r, ps_i = pr[nb - 1::-1].transpose(1, 0, 2), pi[nb - 1::-1].transpose(1, 0, 2)
    ps_r = split(both(ps_r, ps_r)).transpose(0, 1, 3, 2, 4)[..., None, :]
    ps_i = split(both(ps_i, ps_i)).transpose(0, 1, 3, 2, 4)[..., None, :]
    pc = (ps_r * b_for_re + ps_i * b_for_im).reshape(tiles, pairs, nb * 2 * c, 2 * p)
    c_for_re = split(both(cre, -cim))[:, :, None]
    c_for_im = split(both(-cim, -cre))[:, :, None]
    pt_r, pt_i = pr[1:].transpose(1, 0, 2), pi[1:].transpose(1, 0, 2)
    pt_r = split(both(pt_r, pt_r)).transpose(0, 1, 3, 2, 4)[..., None, :]
    pt_i = split(both(pt_i, pt_i)).transpose(0, 1, 3, 2, 4)[..., None, :]
    qc = (pt_r * c_for_re + pt_i * c_for_im).reshape(tiles, pairs, nb * 2 * c, 2 * p)
    ct_r, ct_i = cre.transpose(0, 2, 1), cim.transpose(0, 2, 1)
    c0 = jnp.stack([ct_r, -ct_i], axis=1).reshape(tiles, 256 // c, 2 * p, c)
    c0 = c0.transpose(0, 2, 1, 3).reshape(tiles, 2 * p, 256)
    levels = max(1, int(np.ceil(np.log2(max(n_blocks, 2)))))
    a_re, a_im = [pr[nb]], [pi[nb]]
    for _ in range(levels - 1):
        r, i = a_re[-1], a_im[-1]
        a_re.append(r * r - i * i)
        a_im.append(2.0 * r * i)
    a_re, a_im = jnp.stack(a_re, axis=0), jnp.stack(a_im, axis=0)
    a_re = a_re.reshape(levels, tiles, -1).transpose(1, 0, 2)
    a_im = a_im.reshape(levels, tiles, -1).transpose(1, 0, 2)
    return c0.astype(BF16), pc.astype(BF16), qc.astype(BF16), a_re, a_im, d_skip.astype(F32).reshape(1, g * c)


def _k5_kernel(ys_ref, h_ref, p_ref, w1_ref, w2_ref, nffn_ref, up_ref, down_ref, nple_ref, gate_ref,
               proj_ref, nfinal_ref, o_ref, perm_ref):
    blocks = ys_ref.shape[1]
    for s in range(S5_BLOCK):
        for c, lanes in enumerate(_lane_chunks(D_MODEL)):
            perm_ref[c, pl.ds(s, blocks, stride=S5_BLOCK), :] = ys_ref[s, :, lanes].astype(F32)
    yg = jnp.concatenate([perm_ref[c] for c in range(D_MODEL // 128)], axis=1).astype(BF16)
    h = h_ref[...] + _dot(yg, w1_ref[...]) * _sigmoid(_dot(yg, w2_ref[...]))
    h = _tail(h, p_ref, nffn_ref, up_ref, down_ref, nple_ref, gate_ref, proj_ref)
    o_ref[...] = _rms(h, nfinal_ref[...])


def _k5(ys, h, p3d, w1, w2, nffn, up, down, nple, gate, proj, nfinal):
    t = h.shape[0]
    rows = min(TAIL_ROWS, t)
    consts = [w1, w2, nffn, up, down, nple, gate, proj, nfinal]
    return pl.pallas_call(
        _k5_kernel,
        grid=(t // rows,),
        in_specs=[_posmajor_spec(rows), _row_spec(rows, D_MODEL), _layer_spec(rows, 1)]
                 + [_const_spec(c.shape) for c in consts],
        out_specs=_row_spec(rows, D_MODEL),
        out_shape=jax.ShapeDtypeStruct((t, D_MODEL), F32),
        scratch_shapes=[pltpu.VMEM((D_MODEL // 128, rows, 128), F32)],
        compiler_params=_params("arbitrary"),
        name="k5_glu_mlp_ple_norm",
    )(ys, h, p3d, *consts)


def _row(vec):
    return vec.astype(F32).reshape(1, -1)


def _lora_rows(m, first, width):
    return jnp.zeros((D_LORA, width), F32).at[first:first + m.shape[0]].set(m.astype(F32)).astype(BF16)


def kernel(x, p, l0_norm_mix, l0_w_in, l0_shift_mu, l0_w0, l0_w_lora_up, l0_a0, l0_a_lora_up, l0_g_lora_up, l0_k_k, l0_k_a, l0_r_k, l0_ln_w, l0_ln_b, l0_conv_w, l0_w_out, l0_norm_ffn, l0_ffn_up, l0_ffn_down, l0_norm_ple, l0_ple_gate, l0_ple_proj, l1_norm_mix, l1_lambda_re, l1_lambda_im, l1_log_step, l1_b_re, l1_b_im, l1_c_re, l1_c_im, l1_d_skip, l1_glu_w1, l1_glu_w2, l1_norm_ffn, l1_ffn_up, l1_ffn_down, l1_norm_ple, l1_ple_gate, l1_ple_proj, norm_final):
    bsz, t, _ = x.shape
    assert bsz == 1 and t % K1_ROWS == 0 and t % TAIL_ROWS == 0
    x2d = x.reshape(t, D_MODEL)
    p3d = p.reshape(p.shape[0], t, D_PLE)

    at, bt, kt, rt, v, bg, kg, gam, g, bv, yb = _k1(
        x2d, _row(l0_norm_mix), l0_w_in.astype(BF16), _row(l0_shift_mu[:D_IN_A]), _row(l0_w0),
        _lora_rows(l0_w_lora_up, 0, D_RWKV), _row(l0_a0), _lora_rows(l0_a_lora_up, LORA_W, D_RWKV),
        _lora_rows(l0_g_lora_up, LORA_W + LORA_A, D_RWKV), _row(l0_k_k), _row(l0_k_a), _row(l0_r_k),
        l0_conv_w.astype(F32))
    y = _k2(at, bt, kt, rt, v, bg, kg, gam)
    h, hn = _k3(y, g, bv, yb, x2d, p3d, _row(l0_ln_w), _row(l0_ln_b), l0_w_out.astype(BF16),
                _row(l0_norm_ffn), l0_ffn_up.astype(BF16), l0_ffn_down.astype(BF16),
                _row(l0_norm_ple), l0_ple_gate.astype(BF16), l0_ple_proj.astype(BF16), _row(l1_norm_mix))

    tables = _s5_tables(l1_lambda_re, l1_lambda_im, l1_log_step, l1_b_re, l1_b_im, l1_c_re, l1_c_im,
                        l1_d_skip, t // S5_BLOCK)
    ys = _k4(hn, *tables)

    out = _k5(ys, h, p3d, l1_glu_w1.astype(BF16), l1_glu_w2.astype(BF16), _row(l1_norm_ffn),
              l1_ffn_up.astype(BF16), l1_ffn_down.astype(BF16), _row(l1_norm_ple),
              l1_ple_gate.astype(BF16), l1_ple_proj.astype(BF16), _row(norm_final))
    return out.reshape(bsz, t, D_MODEL)
```

```python
import numpy as np
import jax
import jax.numpy as jnp
from jax import lax
from jax.experimental import pallas as pl
from jax.experimental.pallas import tpu as pltpu

F32 = jnp.float32
BF16 = jnp.bfloat16

D_MODEL = 1024
N_HEADS = 8
HEAD = 64
D_RWKV = N_HEADS * HEAD
D_CONV = D_MODEL - D_RWKV
LORA_W, LORA_A, LORA_G = 64, 64, 128
D_LORA = LORA_W + LORA_A + LORA_G
D_IN_A = 3 * D_RWKV + D_LORA
D_IN = D_IN_A + 3 * D_CONV
GN_EPS = 64e-5
RMS_EPS = 1e-6
SSM_GROUPS, SSM_GROUP, SSM_STATE = 64, 16, 64
D_FF = 4 * D_MODEL
D_PLE = 256

VMEM_LIMIT = 56 * 1024 * 1024

RWKV_CHUNK = 64
K1_ROWS = 256
K2_CHUNKS = 4
TAIL_ROWS = 512
FF_CHUNK = 1024
S5_BLOCK = 16
S5_ROWS = 512


def _dot(a, b):
    return jnp.dot(a, b, preferred_element_type=F32)


def _dot_nt(a, b):
    return lax.dot_general(a, b, (((1,), (1,)), ((), ())), preferred_element_type=F32)


def _split_bf16(x, n):
    pieces = []
    for _ in range(n - 1):
        p = x.astype(BF16)
        pieces.append(p)
        x = x - p.astype(F32)
    pieces.append(x.astype(BF16))
    return pieces


def _rms(x, gain):
    return x * lax.rsqrt(jnp.mean(x * x, axis=-1, keepdims=True) + RMS_EPS) * gain


def _sigmoid(x):
    return 1.0 / (1.0 + jnp.exp(-x))


def _softplus(x):
    return jnp.maximum(x, 0.0) + jnp.log(1.0 + jnp.exp(-jnp.abs(x)))


def _gelu_tanh(x):
    return 0.5 * x * (1.0 + jnp.tanh(np.sqrt(2.0 / np.pi).astype(np.float32) * (x + 0.044715 * (x * x * x))))


def _const_spec(shape):
    nd = len(shape)
    return pl.BlockSpec(shape, lambda *_: (0,) * nd, pipeline_mode=pl.Buffered(1))


def _row_spec(rows, cols):
    return pl.BlockSpec((rows, cols), lambda i: (i, 0))


def _layer_spec(rows, layer):
    return pl.BlockSpec((None, rows, D_PLE), lambda i: (layer, i, 0))


def _posmajor_spec(rows):
    return pl.BlockSpec((S5_BLOCK, rows // S5_BLOCK, D_MODEL), lambda i: (0, i, 0))


def _lane_chunks(width):
    return [slice(c, c + 128) for c in range(0, width, 128)]


def _params(*sem):
    return pltpu.CompilerParams(dimension_semantics=sem, vmem_limit_bytes=VMEM_LIMIT)


def _head_ones(scale):
    lane = np.arange(D_RWKV)
    return jnp.asarray(scale * ((lane[:, None] // HEAD) == (lane[None, :] // HEAD)), BF16)


def _k1_kernel(x_ref, gmix_ref, win_ref, mu_ref, w0_ref, ww_ref, a0_ref, wa_ref, wg_ref, kk_ref, ka_ref,
               rk_ref, cw_ref, tri_ref, sel_ref, hsum_ref,
               at_ref, rt_ref, v_ref, btk_ref, bgk_ref, gam_ref, g_ref, bv_ref, yb_ref,
               zs_ref, us_ref):
    rows = x_ref.shape[0]

    @pl.when(pl.program_id(0) == 0)
    def _():
        zs_ref[0:8, :] = jnp.zeros((8, D_IN_A), F32)
        us_ref[0:8, :] = jnp.zeros((8, D_CONV), F32)

    xn = _rms(x_ref[...], gmix_ref[...])
    z = _dot(xn.astype(BF16), win_ref[...])

    za = z[:, :D_IN_A]
    zs_ref[8:8 + rows, :] = za
    za_prev = zs_ref[7:7 + rows, :]
    zs_ref[7:8, :] = za[rows - 1:rows, :]
    za = za + mu_ref[...] * (za_prev - za)

    r = za[:, 0:D_RWKV]
    k = za[:, D_RWKV:2 * D_RWKV]
    v = za[:, 2 * D_RWKV:3 * D_RWKV]
    lora_in = za[:, 3 * D_RWKV:]

    w_log = -_softplus(-(w0_ref[...] + _dot(jnp.tanh(lora_in).astype(BF16), ww_ref[...]))) - 0.5
    logw = -jnp.exp(w_log)
    lr = _sigmoid(a0_ref[...] + _dot(lora_in.astype(BF16), wa_ref[...]))
    g_ref[...] = _dot(_sigmoid(lora_in).astype(BF16), wg_ref[...])

    lw = _split_bf16(logw, 3)
    c_incl = sum(_dot(tri_ref[...], p) for p in lw[:2])
    c_chunk = sum(_dot(sel_ref[...], p) for p in lw)
    gam_ref[...] = jnp.exp(c_chunk)
    chunk_id = lax.broadcasted_iota(jnp.int32, (rows, 1), 0) // RWKV_CHUNK
    c_tot = c_chunk[0:1, :]
    for i in range(1, rows // RWKV_CHUNK):
        c_tot = jnp.where(chunk_id == i, c_chunk[i:i + 1, :], c_tot)
    e_neg = jnp.exp(-c_incl)
    e_rest = jnp.exp(c_tot - c_incl)

    kk = k * kk_ref[...]
    kk = kk / jnp.maximum(jnp.sqrt(_dot((kk * kk).astype(BF16), hsum_ref[...])), 1e-12)
    kmod = k * (1.0 + (lr - 1.0) * ka_ref[...])
    at_ref[...] = (-kk * jnp.exp(c_incl - logw)).astype(BF16)
    rt_ref[...] = (r * jnp.exp(c_incl)).astype(BF16)

    def chunk_major_t(parts):
        n = RWKV_CHUNK
        stacked = [p[c * n:(c + 1) * n] for c in range(rows // n) for p in parts]
        return jnp.concatenate(stacked, axis=0).T.astype(BF16)

    b_t, k_t = kk * lr * e_neg, kmod * e_neg
    btk_ref[...] = chunk_major_t([b_t, b_t, k_t, k_t])
    bgk_ref[...] = chunk_major_t([kk * lr * e_rest, kmod * e_rest])
    v_ref[...] = v.astype(BF16)
    bv_ref[...] = _dot((r * kmod * rk_ref[...]).astype(BF16), hsum_ref[...]) * v

    zb = z[:, D_IN_A:]
    b_gate = zb[:, :D_CONV]
    u = zb[:, D_CONV:2 * D_CONV] * zb[:, 2 * D_CONV:]
    us_ref[8:8 + rows, :] = u
    u1 = us_ref[7:7 + rows, :]
    u2 = us_ref[6:6 + rows, :]
    us_ref[6:8, :] = u[rows - 2:rows, :]
    cw = cw_ref[...]
    yb_ref[...] = (b_gate * (cw[0:1] * u + cw[1:2] * u1 + cw[2:3] * u2)).astype(BF16)


def _k1(x2d, gmix, win, mu, w0, ww, a0, wa, wg, k_k, k_a, r_k, conv_w):
    t = x2d.shape[0]
    rows = K1_ROWS
    r_idx = np.arange(rows)
    same = (r_idx[:, None] // RWKV_CHUNK) == (r_idx[None, :] // RWKV_CHUNK)
    tri = jnp.asarray(same & (r_idx[None, :] <= r_idx[:, None]), BF16)
    sel = jnp.asarray(np.arange(8)[:, None] == (r_idx[None, :] // RWKV_CHUNK), BF16)

    consts = [gmix, win, mu, w0, ww, a0, wa, wg, k_k, k_a, r_k, conv_w, tri, sel, _head_ones(1.0)]
    wide = jax.ShapeDtypeStruct((t, D_RWKV), BF16)
    out_shape = [wide] * 3 + [
        jax.ShapeDtypeStruct((D_RWKV, 4 * t), BF16),
        jax.ShapeDtypeStruct((D_RWKV, 2 * t), BF16),
        jax.ShapeDtypeStruct((t // rows, 8, D_RWKV), F32),
        jax.ShapeDtypeStruct((t, D_RWKV), F32),
        jax.ShapeDtypeStruct((t, D_RWKV), F32),
        jax.ShapeDtypeStruct((t, D_CONV), BF16),
    ]
    out_specs = [_row_spec(rows, D_RWKV)] * 3 + [
        pl.BlockSpec((D_RWKV, 4 * rows), lambda i: (0, i)), pl.BlockSpec((D_RWKV, 2 * rows), lambda i: (0, i)),
        pl.BlockSpec((None, 8, D_RWKV), lambda i: (i, 0, 0)),
        _row_spec(rows, D_RWKV), _row_spec(rows, D_RWKV), _row_spec(rows, D_CONV)]
    return pl.pallas_call(
        _k1_kernel,
        grid=(t // rows,),
        in_specs=[_row_spec(rows, D_MODEL)] + [_const_spec(c.shape) for c in consts],
        out_specs=out_specs,
        out_shape=out_shape,
        scratch_shapes=[pltpu.VMEM((rows + 8, D_IN_A), F32), pltpu.VMEM((rows + 8, D_CONV), F32)],
        compiler_params=_params("arbitrary"),
        name="k1_inproj_prep",
    )(x2d, *consts)


def _k2_kernel(at_ref, rt_ref, v_ref, btk_ref, bgk_ref, gam_ref, y_ref, s_ref):
    n = RWKV_CHUNK
    pw = 2 * HEAD
    step = pl.program_id(0)

    @pl.when(step == 0)
    def _():
        s_ref[...] = jnp.zeros_like(s_ref)

    zero = jnp.zeros((), BF16)
    iota = lambda shape, d: lax.broadcasted_iota(jnp.int32, shape, d)
    same = iota((pw, pw), 0) // HEAD == iota((pw, pw), 1) // HEAD
    same_wide = jnp.concatenate([same, same], axis=1)
    diag = iota((pw, pw), 0) == iota((pw, pw), 1)
    eye2 = jnp.where(iota((n, pw), 0) == iota((n, pw), 1) % n, 1.0, 0.0).astype(F32)
    row2, col2 = iota((2 * n, 2 * pw), 0), iota((2 * n, 2 * pw), 1) % n
    tri2 = col2 <= jnp.where(row2 < n, row2 - 1, row2 - n)
    sub = iota((8, D_RWKV), 0)
    first = (step * K2_CHUNKS) % (K1_ROWS // n)

    def bd(m):
        return jnp.where(same, jnp.concatenate([m, m], axis=0), zero)

    units = [(c, p) for c in range(K2_CHUNKS) for p in range(N_HEADS // 2)]
    tile = lambda ref, u: ref[u[0] * n:(u[0] + 1) * n, u[1] * pw:(u[1] + 1) * pw]
    a = {u: tile(at_ref, u) for u in units}
    r = {u: tile(rt_ref, u) for u in units}
    v = {u: tile(v_ref, u) for u in units}
    gm = {}
    for u in units:
        c, p = u
        rhs = jnp.where(same_wide, btk_ref[p * pw:(p + 1) * pw, c * 4 * n:(c + 1) * 4 * n], zero)
        gm[u] = jnp.where(tri2, _dot(jnp.concatenate([a[u], r[u]], axis=0), rhs), 0.0)
    gmb = {u: gm[u].astype(BF16) for u in units}
    wv = {u: _dot(gmb[u][:, pw:], bd(v[u])) for u in units}

    q = {u: gm[u][:n, :pw] for u in units}
    tinv = {u: eye2 + q[u] for u in units}
    for u in units:
        qb = q[u].astype(BF16)
        q[u] = _dot(qb, bd(qb))
    for level in range(1, 6):
        for u in units:
            qb = q[u].astype(BF16)
            if level < 5:
                tq = _dot(jnp.concatenate([tinv[u].astype(BF16), qb], axis=0), bd(qb))
                tinv[u], q[u] = tinv[u] + tq[:n], tq[n:]
            else:
                tinv[u] = tinv[u] + _dot(tinv[u].astype(BF16), bd(qb))

    au, ry, mc = {}, {}, {}
    for u in units:
        rhs = jnp.concatenate([bd(a[u]), bd(wv[u][:n].astype(BF16))], axis=1)
        au[u] = _dot(tinv[u].astype(BF16), rhs).astype(BF16)
    for u in units:
        rhs = jnp.concatenate([bd(au[u][:, :pw]), bd(au[u][:, pw:])], axis=1)
        ry[u] = jnp.concatenate([r[u].astype(F32), wv[u][n:]], axis=1) + _dot(gmb[u][n:, :pw], rhs)
    for u in units:
        c, p = u
        lhs = bgk_ref[p * pw:(p + 1) * pw, c * 2 * n:(c + 1) * 2 * n]
        rhs = jnp.concatenate([au[u], jnp.concatenate([jnp.zeros((n, pw), BF16), v[u]], axis=1)], axis=0)
        mc[u] = jnp.where(same_wide, _dot(lhs, rhs), 0.0)
    for u in units:
        c, p = u
        gam_row = jnp.sum(jnp.where(sub == first + c, gam_ref[...], 0.0), axis=0, keepdims=True)
        m_t = jnp.where(diag, gam_row[:, p * pw:(p + 1) * pw], 0.0) + mc[u][:, :pw]
        mr = _dot(jnp.concatenate([m_t, ry[u][:, :pw]], axis=0).astype(BF16), s_ref[p].astype(BF16))
        s_ref[p] = mr[:pw] + mc[u][:, pw:]
        y_ref[c * n:(c + 1) * n, p * pw:(p + 1) * pw] = mr[pw:] + ry[u][:, pw:]


def _k2(at, rt, v, btk, bgk, gam):
    t = at.shape[0]
    rows = RWKV_CHUNK * K2_CHUNKS
    per = K1_ROWS // rows
    seq = _row_spec(rows, D_RWKV)
    return pl.pallas_call(
        _k2_kernel,
        grid=(t // rows,),
        in_specs=[seq] * 3 + [pl.BlockSpec((D_RWKV, 4 * rows), lambda i: (0, i)),
                              pl.BlockSpec((D_RWKV, 2 * rows), lambda i: (0, i)),
                              pl.BlockSpec((None, 8, D_RWKV), lambda i: (i // per, 0, 0))],
        out_specs=seq,
        out_shape=jax.ShapeDtypeStruct((t, D_RWKV), F32),
        scratch_shapes=[pltpu.VMEM((N_HEADS // 2, 2 * HEAD, 2 * HEAD), F32)],
        compiler_params=_params("arbitrary"),
        name="k2_rwkv_chunked",
    )(at, rt, v, btk, bgk, gam)


def _tail(h, p_ref, nffn_ref, up_ref, down_ref, nple_ref, gate_ref, proj_ref):
    hn = _rms(h, nffn_ref[...]).astype(BF16)
    mlp = None
    for c in range(D_FF // FF_CHUNK):
        a = jnp.maximum(_dot(hn, up_ref[:, c * FF_CHUNK:(c + 1) * FF_CHUNK]), 0.0)
        part = _dot((a * a).astype(BF16), down_ref[c * FF_CHUNK:(c + 1) * FF_CHUNK, :])
        mlp = part if mlp is None else mlp + part
    h = h + mlp
    gate = _sigmoid(_dot(_rms(h, nple_ref[...]).astype(BF16), gate_ref[...]))
    return h + _dot(p_ref[...].astype(BF16), proj_ref[...]) * gate


def _k3_kernel(y_ref, g_ref, bv_ref, yb_ref, x_ref, p_ref, gnmean_ref, lnw_ref, lnb_ref, wout_ref,
               nffn_ref, up_ref, down_ref, nple_ref, gate_ref, proj_ref, nnext_ref, h_ref, hn_ref, perm_ref):
    y = y_ref[...]
    d = y - _dot(y.astype(BF16), gnmean_ref[...])
    var = _dot((d * d).astype(BF16), gnmean_ref[...])
    yn = d * lax.rsqrt(var + GN_EPS) * lnw_ref[...] + lnb_ref[...]
    ya = ((yn + bv_ref[...]) * g_ref[...]).astype(BF16)
    h = x_ref[...] + _dot(ya, wout_ref[0:D_RWKV, :]) + _dot(yb_ref[...], wout_ref[D_RWKV:, :])
    h = _tail(h, p_ref, nffn_ref, up_ref, down_ref, nple_ref, gate_ref, proj_ref)
    h_ref[...] = h
    hn = _rms(h, nnext_ref[...])
    blocks = hn.shape[0] // S5_BLOCK
    for c, lanes in enumerate(_lane_chunks(D_MODEL)):
        perm_ref[c] = hn[:, lanes]
    for s in range(S5_BLOCK):
        for c, lanes in enumerate(_lane_chunks(D_MODEL)):
            hn_ref[s, :, lanes] = perm_ref[c, pl.ds(s, blocks, stride=S5_BLOCK), :].astype(BF16)


def _k3(y, g, bv, yb, x2d, p3d, ln_w, ln_b, wout, nffn, up, down, nple, gate, proj, nnext):
    t = x2d.shape[0]
    rows = min(TAIL_ROWS, t)
    consts = [_head_ones(1.0 / HEAD), ln_w, ln_b, wout, nffn, up, down, nple, gate, proj, nnext]
    return pl.pallas_call(
        _k3_kernel,
        grid=(t // rows,),
        in_specs=[_row_spec(rows, D_RWKV), _row_spec(rows, D_RWKV), _row_spec(rows, D_RWKV),
                  _row_spec(rows, D_CONV), _row_spec(rows, D_MODEL), _layer_spec(rows, 0)]
                 + [_const_spec(c.shape) for c in consts],
        out_specs=[_row_spec(rows, D_MODEL), _posmajor_spec(rows)],
        out_shape=[jax.ShapeDtypeStruct((t, D_MODEL), F32),
                   jax.ShapeDtypeStruct((S5_BLOCK, t // S5_BLOCK, D_MODEL), BF16)],
        scratch_shapes=[pltpu.VMEM((D_MODEL // 128, rows, 128), F32)],
        compiler_params=_params("arbitrary"),
        name="k3_mix_out_mlp_ple",
    )(y, g, bv, yb, x2d, p3d, *consts)


def _shift_rows(x, s):
    n = x.shape[0]
    if s % 8 == 0:
        return jnp.concatenate([jnp.zeros((s, x.shape[1]), x.dtype), x[:n - s]], axis=0)
    keep = lax.broadcasted_iota(jnp.int32, x.shape, 0) >= s
    return jnp.where(keep, pltpu.roll(x, s, axis=0), 0.0)


def _k4_kernel(x_ref, c0_ref, pc_ref, qc_ref, are_ref, aim_ref, dsk_ref, y_ref, wt_ref, carry_ref):
    nb, rows, lanes = x_ref.shape
    gl = lanes // SSM_GROUP
    sw = 2 * SSM_STATE
    zero = jnp.zeros((), BF16)

    pw = 2 * SSM_GROUP
    npair = lanes // pw

    @pl.when(pl.program_id(1) == 0)
    def _():
        carry_ref[...] = jnp.zeros_like(carry_ref)
        c_row = lax.broadcasted_iota(jnp.int32, (gl * sw, lanes), 0) // sw
        c_col = lax.broadcasted_iota(jnp.int32, (gl * sw, lanes), 1) // SSM_GROUP
        cbd = jnp.where(c_row == c_col, jnp.concatenate([c0_ref[...]] * gl, axis=0), zero)
        b_row = lax.broadcasted_iota(jnp.int32, (lanes, gl * sw), 0) // SSM_GROUP
        b_col = lax.broadcasted_iota(jnp.int32, (lanes, gl * sw), 1) // sw
        for s in range(nb):
            lam_b = jnp.concatenate([pc_ref[j][s * pw:(s + 1) * pw, :] for j in range(npair)], axis=0)
            bbd = jnp.where(b_row == b_col, jnp.concatenate([lam_b] * gl, axis=1), zero)
            wt_ref[s * lanes:(s + 1) * lanes, :] = _dot(bbd, cbd).astype(BF16)

    slot = lax.broadcasted_iota(jnp.int32, (1, 128), 1) // pw

    def gather(src_of, src_slot_of, count):
        cols = []
        for w in range(count // 4):
            col = None
            for kk in range(4):
                src = src_of(4 * w + kk)
                shift = (pw * (kk - src_slot_of(4 * w + kk))) % 128
                src = pltpu.roll(src, shift, axis=1) if shift else src
                col = src if col is None else jnp.where(slot == kk, src, col)
            cols.append(col)
        return jnp.concatenate(cols, axis=1)

    ns = SSM_STATE
    p_same = ((lax.broadcasted_iota(jnp.int32, (nb * pw, 2 * ns), 0) // SSM_GROUP) % 2
              == lax.broadcasted_iota(jnp.int32, (nb * pw, 2 * ns), 1) // ns)
    parts_r, parts_i = [], []
    for j in range(npair):
        lhs = gather(lambda s: x_ref[s][:, 128 * (j // 4):128 * (j // 4 + 1)], lambda s: j % 4, nb)
        pcj = pc_ref[j]
        parts_r.append(_dot(lhs, jnp.where(p_same, jnp.concatenate([pcj[:, :ns]] * 2, axis=1), zero)))
        parts_i.append(_dot(lhs, jnp.where(p_same, jnp.concatenate([pcj[:, ns:]] * 2, axis=1), zero)))
    sr = jnp.concatenate(parts_r, axis=1)
    si = jnp.concatenate(parts_i, axis=1)

    first = lax.broadcasted_iota(jnp.int32, sr.shape, 0) == 0
    hr_in, hi_in = carry_ref[0:1, :], carry_ref[8:9, :]
    ar, ai = are_ref[0:1, :], aim_ref[0:1, :]
    sr = sr + jnp.where(first, ar * hr_in - ai * hi_in, 0.0)
    si = si + jnp.where(first, ar * hi_in + ai * hr_in, 0.0)
    for k in range(are_ref.shape[0]):
        if (1 << k) >= rows:
            break
        ar, ai = are_ref[k:k + 1, :], aim_ref[k:k + 1, :]
        tr, ti = _shift_rows(sr, 1 << k), _shift_rows(si, 1 << k)
        sr, si = sr + ar * tr - ai * ti, si + ar * ti + ai * tr
    carry_ref[0:1, :] = sr[rows - 1:rows, :]
    carry_ref[8:9, :] = si[rows - 1:rows, :]
    hbr = jnp.where(first, hr_in, _shift_rows(sr, 1)).astype(BF16)
    hbi = jnp.where(first, hi_in, _shift_rows(si, 1)).astype(BF16)

    y_in = []
    for j in range(npair):
        qcj = qc_ref[j]
        q_re = jnp.where(p_same, jnp.concatenate([qcj[:, :ns]] * 2, axis=1), zero)
        q_im = jnp.where(p_same, jnp.concatenate([qcj[:, ns:]] * 2, axis=1), zero)
        cols = slice(j * 2 * ns, (j + 1) * 2 * ns)
        y_in.append((_dot_nt(hbr[:, cols], q_re) + _dot_nt(hbi[:, cols], q_im)).astype(BF16))
    dsk = dsk_ref[...]
    for t in range(nb):
        xcat = jnp.concatenate([x_ref[s] for s in range(t + 1)], axis=1)
        y_t = gather(lambda j: y_in[j][:, 128 * (t // 4):128 * (t // 4 + 1)], lambda j: t % 4, npair)
        y = _dot(xcat, wt_ref[(nb - 1 - t) * lanes:, :]) + y_t.astype(F32)
        y_ref[t] = _gelu_tanh(y + dsk * x_ref[t].astype(F32)).astype(BF16)


def _k4(xs, c0, pc, qc, a_re, a_im, dsk):
    nb, m, d = xs.shape
    lanes = 256
    rows = min(S5_ROWS, m)
    return pl.pallas_call(
        _k4_kernel,
        grid=(d // lanes, m // rows),
        in_specs=[pl.BlockSpec((nb, rows, lanes), lambda q, i: (0, i, q)),
                  pl.BlockSpec((None,) + c0.shape[1:], lambda q, i: (q, 0, 0)),
                  pl.BlockSpec((None,) + pc.shape[1:], lambda q, i: (q, 0, 0, 0)),
                  pl.BlockSpec((None,) + qc.shape[1:], lambda q, i: (q, 0, 0, 0)),
                  pl.BlockSpec((None,) + a_re.shape[1:], lambda q, i: (q, 0, 0)),
                  pl.BlockSpec((None,) + a_im.shape[1:], lambda q, i: (q, 0, 0)),
                  pl.BlockSpec((1, lanes), lambda q, i: (0, q))],
        out_specs=pl.BlockSpec((nb, rows, lanes), lambda q, i: (0, i, q)),
        out_shape=jax.ShapeDtypeStruct((nb, m, d), BF16),
        scratch_shapes=[pltpu.VMEM((nb * lanes, lanes), BF16),
                        pltpu.VMEM((16, SSM_STATE * (lanes // SSM_GROUP)), F32)],
        compiler_params=_params("arbitrary", "arbitrary"),
        name="k4_s5",
    )(xs, c0, pc, qc, a_re, a_im, dsk)


def _s5_tables(lam_re, lam_im, log_step, b_re, b_im, c_re, c_im, d_skip, n_blocks):
    g, p, c, nb = SSM_GROUPS, SSM_STATE, SSM_GROUP, S5_BLOCK
    lre = jnp.minimum(lam_re.astype(F32), -1e-4)
    lim = lam_im.astype(F32)
    step = jnp.exp(log_step.astype(F32))[:, None]
    ar, ai = lre * step, lim * step
    n = jnp.arange(nb + 1, dtype=F32)[:, None, None]
    mag = jnp.exp(n * ar)
    pr, pi = mag * jnp.cos(n * ai), mag * jnp.sin(n * ai)
    nr, ni = pr[1] - 1.0, pi[1]
    den = lre * lre + lim * lim
    qr, qi = (nr * lre + ni * lim) / den, (ni * lre - nr * lim) / den
    bre, bim = b_re.astype(F32), b_im.astype(F32)
    bbr = qr[..., None] * bre - qi[..., None] * bim
    bbi = qr[..., None] * bim + qi[..., None] * bre
    cre, cim = c_re.astype(F32), c_im.astype(F32)
    tiles, pairs = g * c // 256, 256 // (2 * c)
    split = lambda m: m.reshape((tiles, pairs, 2) + m.shape[1:])
    both = lambda a, b: jnp.concatenate([a, b], axis=-1)
    bt_r, bt_i = bbr.transpose(0, 2, 1), bbi.transpose(0, 2, 1)
    b_for_re = split(both(bt_r, bt_i))[:, :, None]
    b_for_im = split(both(-bt_i, bt_r))[:, :, None]
    ps_r, ps_i = pr[nb - 1::-1].transpose(1, 0, 2), pi[nb - 1::-1].transpose(1, 0, 2)
    ps_r = split(both(ps_r, ps_r)).transpose(0, 1, 3, 2, 4)[..., None, :]
    ps_i = split(both(ps_i, ps_i)).transpose(0, 1, 3, 2, 4)[..., None, :]
    pc = (ps_r * b_for_re + ps_i * b_for_im).reshape(tiles, pairs, nb * 2 * c, 2 * p)
    c_for_re = split(both(cre, -cim))[:, :, None]
    c_for_im = split(both(-cim, -cre))[:, :, None]
    pt_r, pt_i = pr[1:].transpose(1, 0, 2), pi[1:].transpose(1, 0, 2)
    pt_r = split(both(pt_r, pt_r)).transpose(0, 1, 3, 2, 4)[..., None, :]
    pt_i = split(both(pt_i, pt_i)).transpose(0, 1, 3, 2, 4)[..., None, :]
    qc = (pt_r * c_for_re + pt_i * c_for_im).reshape(tiles, pairs, nb * 2 * c, 2 * p)
    ct_r, ct_i = cre.transpose(0, 2, 1), cim.transpose(0, 2, 1)
    c0 = jnp.stack([ct_r, -ct_i], axis=1).reshape(tiles, 256 // c, 2 * p, c)
    c0 = c0.transpose(0, 2, 1, 3).reshape(tiles, 2 * p, 256)
    levels = max(1, int(np.ceil(np.log2(max(n_blocks, 2)))))
    a_re, a_im = [pr[nb]], [pi[nb]]
    for _ in range(levels - 1):
        r, i = a_re[-1], a_im[-1]
        a_re.append(r * r - i * i)
        a_im.append(2.0 * r * i)
    a_re, a_im = jnp.stack(a_re, axis=0), jnp.stack(a_im, axis=0)
    a_re = a_re.reshape(levels, tiles, -1).transpose(1, 0, 2)
    a_im = a_im.reshape(levels, tiles, -1).transpose(1, 0, 2)
    return c0.astype(BF16), pc.astype(BF16), qc.astype(BF16), a_re, a_im, d_skip.astype(F32).reshape(1, g * c)


def _k5_kernel(ys_ref, h_ref, p_ref, w1_ref, w2_ref, nffn_ref, up_ref, down_ref, nple_ref, gate_ref,
               proj_ref, nfinal_ref, o_ref, perm_ref):
    blocks = ys_ref.shape[1]
    for s in range(S5_BLOCK):
        for c, lanes in enumerate(_lane_chunks(D_MODEL)):
            perm_ref[c, pl.ds(s, blocks, stride=S5_BLOCK), :] = ys_ref[s, :, lanes].astype(F32)
    yg = jnp.concatenate([perm_ref[c] for c in range(D_MODEL // 128)], axis=1).astype(BF16)
    h = h_ref[...] + _dot(yg, w1_ref[...]) * _sigmoid(_dot(yg, w2_ref[...]))
    h = _tail(h, p_ref, nffn_ref, up_ref, down_ref, nple_ref, gate_ref, proj_ref)
    o_ref[...] = _rms(h, nfinal_ref[...])


def _k5(ys, h, p3d, w1, w2, nffn, up, down, nple, gate, proj, nfinal):
    t = h.shape[0]
    rows = min(TAIL_ROWS, t)
    consts = [w1, w2, nffn, up, down, nple, gate, proj, nfinal]
    return pl.pallas_call(
        _k5_kernel,
        grid=(t // rows,),
        in_specs=[_posmajor_spec(rows), _row_spec(rows, D_MODEL), _layer_spec(rows, 1)]
                 + [_const_spec(c.shape) for c in consts],
        out_specs=_row_spec(rows, D_MODEL),
        out_shape=jax.ShapeDtypeStruct((t, D_MODEL), F32),
        scratch_shapes=[pltpu.VMEM((D_MODEL // 128, rows, 128), F32)],
        compiler_params=_params("arbitrary"),
        name="k5_glu_mlp_ple_norm",
    )(ys, h, p3d, *consts)


def _row(vec):
    return vec.astype(F32).reshape(1, -1)


def _lora_rows(m, first, width):
    return jnp.zeros((D_LORA, width), F32).at[first:first + m.shape[0]].set(m.astype(F32)).astype(BF16)


def kernel(x, p, l0_norm_mix, l0_w_in, l0_shift_mu, l0_w0, l0_w_lora_up, l0_a0, l0_a_lora_up, l0_g_lora_up, l0_k_k, l0_k_a, l0_r_k, l0_ln_w, l0_ln_b, l0_conv_w, l0_w_out, l0_norm_ffn, l0_ffn_up, l0_ffn_down, l0_norm_ple, l0_ple_gate, l0_ple_proj, l1_norm_mix, l1_lambda_re, l1_lambda_im, l1_log_step, l1_b_re, l1_b_im, l1_c_re, l1_c_im, l1_d_skip, l1_glu_w1, l1_glu_w2, l1_norm_ffn, l1_ffn_up, l1_ffn_down, l1_norm_ple, l1_ple_gate, l1_ple_proj, norm_final):
    bsz, t, _ = x.shape
    assert bsz == 1 and t % K1_ROWS == 0 and t % TAIL_ROWS == 0
    x2d = x.reshape(t, D_MODEL)
    p3d = p.reshape(p.shape[0], t, D_PLE)

    at, rt, v, btk, bgk, gam, g, bv, yb = _k1(
        x2d, _row(l0_norm_mix), l0_w_in.astype(BF16), _row(l0_shift_mu[:D_IN_A]), _row(l0_w0),
        _lora_rows(l0_w_lora_up, 0, D_RWKV), _row(l0_a0), _lora_rows(l0_a_lora_up, LORA_W, D_RWKV),
        _lora_rows(l0_g_lora_up, LORA_W + LORA_A, D_RWKV), _row(l0_k_k), _row(l0_k_a), _row(l0_r_k),
        l0_conv_w.astype(F32))
    y = _k2(at, rt, v, btk, bgk, gam)
    h, hn = _k3(y, g, bv, yb, x2d, p3d, _row(l0_ln_w), _row(l0_ln_b), l0_w_out.astype(BF16),
                _row(l0_norm_ffn), l0_ffn_up.astype(BF16), l0_ffn_down.astype(BF16),
                _row(l0_norm_ple), l0_ple_gate.astype(BF16), l0_ple_proj.astype(BF16), _row(l1_norm_mix))

    tables = _s5_tables(l1_lambda_re, l1_lambda_im, l1_log_step, l1_b_re, l1_b_im, l1_c_re, l1_c_im,
                        l1_d_skip, t // S5_BLOCK)
    ys = _k4(hn, *tables)

    out = _k5(ys, h, p3d, l1_glu_w1.astype(BF16), l1_glu_w2.astype(BF16), _row(l1_norm_ffn),
              l1_ffn_up.astype(BF16), l1_ffn_down.astype(BF16), _row(l1_norm_ple),
              l1_ple_gate.astype(BF16), l1_ple_proj.astype(BF16), _row(norm_final))
    return out.reshape(bsz, t, D_MODEL)
```

```python
import numpy as np
import jax
import jax.numpy as jnp
from jax import lax
from jax.experimental import pallas as pl
from jax.experimental.pallas import tpu as pltpu

F32 = jnp.float32
BF16 = jnp.bfloat16

D_MODEL = 1024
N_HEADS = 8
HEAD = 64
D_RWKV = N_HEADS * HEAD
D_CONV = D_MODEL - D_RWKV
LORA_W, LORA_A, LORA_G = 64, 64, 128
D_LORA = LORA_W + LORA_A + LORA_G
D_IN_A = 3 * D_RWKV + D_LORA
D_IN = D_IN_A + 3 * D_CONV
GN_EPS = 64e-5
RMS_EPS = 1e-6
SSM_GROUPS, SSM_GROUP, SSM_STATE = 64, 16, 64
D_FF = 4 * D_MODEL
D_PLE = 256

VMEM_LIMIT = 56 * 1024 * 1024

RWKV_CHUNK = 64
K1_ROWS = 512
K2_CHUNKS = 8
TAIL_ROWS = 512
FF_CHUNK = 1024
S5_BLOCK = 16
S5_ROWS = 512
PERM_PITCH = S5_BLOCK + 1


def _dot(a, b):
    return jnp.dot(a, b, preferred_element_type=F32)


def _dot_nt(a, b):
    return lax.dot_general(a, b, (((1,), (1,)), ((), ())), preferred_element_type=F32)


def _split_bf16(x, n):
    pieces = []
    for _ in range(n - 1):
        p = x.astype(BF16)
        pieces.append(p)
        x = x - p.astype(F32)
    pieces.append(x.astype(BF16))
    return pieces


def _rms(x, gain):
    return x * lax.rsqrt(jnp.mean(x * x, axis=-1, keepdims=True) + RMS_EPS) * gain


def _sigmoid(x):
    return 1.0 / (1.0 + jnp.exp(-x))


def _softplus(x):
    return jnp.maximum(x, 0.0) + jnp.log(1.0 + jnp.exp(-jnp.abs(x)))


def _gelu_tanh(x):
    return 0.5 * x * (1.0 + jnp.tanh(np.sqrt(2.0 / np.pi).astype(np.float32) * (x + 0.044715 * (x * x * x))))


def _const_spec(shape):
    nd = len(shape)
    return pl.BlockSpec(shape, lambda *_: (0,) * nd, pipeline_mode=pl.Buffered(1))


def _row_spec(rows, cols):
    return pl.BlockSpec((rows, cols), lambda i: (i, 0))


def _layer_spec(rows, layer):
    return pl.BlockSpec((None, rows, D_PLE), lambda i: (layer, i, 0))


def _posmajor_spec(rows):
    return pl.BlockSpec((S5_BLOCK, rows // S5_BLOCK, D_MODEL), lambda i: (0, i, 0))


def _lane_chunks(width):
    return [slice(c, c + 128) for c in range(0, width, 128)]


def _params(*sem):
    return pltpu.CompilerParams(dimension_semantics=sem, vmem_limit_bytes=VMEM_LIMIT)


def _head_ones(scale):
    lane = np.arange(D_RWKV)
    return jnp.asarray(scale * ((lane[:, None] // HEAD) == (lane[None, :] // HEAD)), BF16)


def _k1_kernel(x_ref, gmix_ref, win_ref, mu_ref, w0_ref, ww_ref, a0_ref, wa_ref, wg_ref, kk_ref, ka_ref,
               rk_ref, cw_ref, tri_ref, sel_ref, hsum_ref,
               at_ref, rt_ref, v_ref, btk_ref, bgk_ref, gam_ref, g_ref, bv_ref, yb_ref,
               zs_ref, us_ref):
    rows = x_ref.shape[0]

    @pl.when(pl.program_id(0) == 0)
    def _():
        zs_ref[0:8, :] = jnp.zeros((8, D_IN_A), F32)
        us_ref[0:8, :] = jnp.zeros((8, D_CONV), F32)

    xn = _rms(x_ref[...], gmix_ref[...])
    z = _dot(xn.astype(BF16), win_ref[...])

    za = z[:, :D_IN_A]
    zs_ref[8:8 + rows, :] = za
    za_prev = zs_ref[7:7 + rows, :]
    zs_ref[7:8, :] = za[rows - 1:rows, :]
    za = za + mu_ref[...] * (za_prev - za)

    r = za[:, 0:D_RWKV]
    k = za[:, D_RWKV:2 * D_RWKV]
    v = za[:, 2 * D_RWKV:3 * D_RWKV]
    lora_in = za[:, 3 * D_RWKV:]

    w_log = -_softplus(-(w0_ref[...] + _dot(jnp.tanh(lora_in).astype(BF16), ww_ref[...]))) - 0.5
    logw = -jnp.exp(w_log)
    lr = _sigmoid(a0_ref[...] + _dot(lora_in.astype(BF16), wa_ref[...]))
    g_ref[...] = _dot(_sigmoid(lora_in).astype(BF16), wg_ref[...])

    lw = _split_bf16(logw, 3)
    c_incl = sum(_dot(tri_ref[...], p) for p in lw[:2])
    c_chunk = sum(_dot(sel_ref[...], p) for p in lw)
    gam_ref[...] = jnp.exp(c_chunk)
    chunk_id = lax.broadcasted_iota(jnp.int32, (rows, 1), 0) // RWKV_CHUNK
    c_tot = c_chunk[0:1, :]
    for i in range(1, rows // RWKV_CHUNK):
        c_tot = jnp.where(chunk_id == i, c_chunk[i:i + 1, :], c_tot)
    e_neg = jnp.exp(-c_incl)
    e_rest = jnp.exp(c_tot - c_incl)

    kk = k * kk_ref[...]
    kk = kk / jnp.maximum(jnp.sqrt(_dot((kk * kk).astype(BF16), hsum_ref[...])), 1e-12)
    kmod = k * (1.0 + (lr - 1.0) * ka_ref[...])
    at_ref[...] = (-kk * jnp.exp(c_incl - logw)).astype(BF16)
    rt_ref[...] = (r * jnp.exp(c_incl)).astype(BF16)

    def chunk_major_t(parts):
        n = RWKV_CHUNK
        stacked = [p[c * n:(c + 1) * n] for c in range(rows // n) for p in parts]
        return jnp.concatenate(stacked, axis=0).T.astype(BF16)

    b_t, k_t = kk * lr * e_neg, kmod * e_neg
    btk_ref[...] = chunk_major_t([b_t, b_t, k_t, k_t])
    bgk_ref[...] = chunk_major_t([kk * lr * e_rest, kmod * e_rest])
    v_ref[...] = v.astype(BF16)
    bv_ref[...] = _dot((r * kmod * rk_ref[...]).astype(BF16), hsum_ref[...]) * v

    zb = z[:, D_IN_A:]
    b_gate = zb[:, :D_CONV]
    u = zb[:, D_CONV:2 * D_CONV] * zb[:, 2 * D_CONV:]
    us_ref[8:8 + rows, :] = u
    u1 = us_ref[7:7 + rows, :]
    u2 = us_ref[6:6 + rows, :]
    us_ref[6:8, :] = u[rows - 2:rows, :]
    cw = cw_ref[...]
    yb_ref[...] = (b_gate * (cw[0:1] * u + cw[1:2] * u1 + cw[2:3] * u2)).astype(BF16)


def _k1(x2d, gmix, win, mu, w0, ww, a0, wa, wg, k_k, k_a, r_k, conv_w):
    t = x2d.shape[0]
    rows = K1_ROWS
    r_idx = np.arange(rows)
    same = (r_idx[:, None] // RWKV_CHUNK) == (r_idx[None, :] // RWKV_CHUNK)
    tri = jnp.asarray(same & (r_idx[None, :] <= r_idx[:, None]), BF16)
    sel = jnp.asarray(np.arange(8)[:, None] == (r_idx[None, :] // RWKV_CHUNK), BF16)

    consts = [gmix, win, mu, w0, ww, a0, wa, wg, k_k, k_a, r_k, conv_w, tri, sel, _head_ones(1.0)]
    wide = jax.ShapeDtypeStruct((t, D_RWKV), BF16)
    out_shape = [wide] * 3 + [
        jax.ShapeDtypeStruct((D_RWKV, 4 * t), BF16),
        jax.ShapeDtypeStruct((D_RWKV, 2 * t), BF16),
        jax.ShapeDtypeStruct((t // rows, 8, D_RWKV), F32),
        jax.ShapeDtypeStruct((t, D_RWKV), F32),
        jax.ShapeDtypeStruct((t, D_RWKV), F32),
        jax.ShapeDtypeStruct((t, D_CONV), BF16),
    ]
    out_specs = [_row_spec(rows, D_RWKV)] * 3 + [
        pl.BlockSpec((D_RWKV, 4 * rows), lambda i: (0, i)), pl.BlockSpec((D_RWKV, 2 * rows), lambda i: (0, i)),
        pl.BlockSpec((None, 8, D_RWKV), lambda i: (i, 0, 0)),
        _row_spec(rows, D_RWKV), _row_spec(rows, D_RWKV), _row_spec(rows, D_CONV)]
    return pl.pallas_call(
        _k1_kernel,
        grid=(t // rows,),
        in_specs=[_row_spec(rows, D_MODEL)] + [_const_spec(c.shape) for c in consts],
        out_specs=out_specs,
        out_shape=out_shape,
        scratch_shapes=[pltpu.VMEM((rows + 8, D_IN_A), F32), pltpu.VMEM((rows + 8, D_CONV), F32)],
        compiler_params=_params("arbitrary"),
        name="k1_inproj_prep",
    )(x2d, *consts)


def _k2_kernel(at_ref, rt_ref, v_ref, btk_ref, bgk_ref, gam_ref, y_ref, s_ref):
    n = RWKV_CHUNK
    pw = 2 * HEAD
    step = pl.program_id(0)

    @pl.when(step == 0)
    def _():
        s_ref[...] = jnp.zeros_like(s_ref)

    zero = jnp.zeros((), BF16)
    iota = lambda shape, d: lax.broadcasted_iota(jnp.int32, shape, d)
    same = iota((pw, pw), 0) // HEAD == iota((pw, pw), 1) // HEAD
    same_wide = jnp.concatenate([same, same], axis=1)
    diag = iota((pw, pw), 0) == iota((pw, pw), 1)
    eye2 = jnp.where(iota((n, pw), 0) == iota((n, pw), 1) % n, 1.0, 0.0).astype(F32)
    row2, col2 = iota((2 * n, 2 * pw), 0), iota((2 * n, 2 * pw), 1) % n
    tri2 = col2 <= jnp.where(row2 < n, row2 - 1, row2 - n)
    sub = iota((8, D_RWKV), 0)
    first = (step * K2_CHUNKS) % (K1_ROWS // n)

    def bd(m):
        return jnp.where(same, jnp.concatenate([m, m], axis=0), zero)

    units = [(c, p) for c in range(K2_CHUNKS) for p in range(N_HEADS // 2)]
    tile = lambda ref, u: ref[u[0] * n:(u[0] + 1) * n, u[1] * pw:(u[1] + 1) * pw]
    a = {u: tile(at_ref, u) for u in units}
    r = {u: tile(rt_ref, u) for u in units}
    v = {u: tile(v_ref, u) for u in units}
    gm = {}
    for u in units:
        c, p = u
        rhs = jnp.where(same_wide, btk_ref[p * pw:(p + 1) * pw, c * 4 * n:(c + 1) * 4 * n], zero)
        gm[u] = jnp.where(tri2, _dot(jnp.concatenate([a[u], r[u]], axis=0), rhs), 0.0)
    gmb = {u: gm[u].astype(BF16) for u in units}
    wv = {u: _dot(gmb[u][:, pw:], bd(v[u])) for u in units}

    q = {u: gm[u][:n, :pw] for u in units}
    tinv = {u: eye2 + q[u] for u in units}
    for u in units:
        qb = q[u].astype(BF16)
        q[u] = _dot(qb, bd(qb))
    for level in range(1, 6):
        for u in units:
            qb = q[u].astype(BF16)
            if level < 5:
                tq = _dot(jnp.concatenate([tinv[u].astype(BF16), qb], axis=0), bd(qb))
                tinv[u], q[u] = tinv[u] + tq[:n], tq[n:]
            else:
                tinv[u] = tinv[u] + _dot(tinv[u].astype(BF16), bd(qb))

    au, ry, mc = {}, {}, {}
    for u in units:
        rhs = jnp.concatenate([bd(a[u]), bd(wv[u][:n].astype(BF16))], axis=1)
        au[u] = _dot(tinv[u].astype(BF16), rhs).astype(BF16)
    for u in units:
        rhs = jnp.concatenate([bd(au[u][:, :pw]), bd(au[u][:, pw:])], axis=1)
        ry[u] = jnp.concatenate([r[u].astype(F32), wv[u][n:]], axis=1) + _dot(gmb[u][n:, :pw], rhs)
    for u in units:
        c, p = u
        lhs = bgk_ref[p * pw:(p + 1) * pw, c * 2 * n:(c + 1) * 2 * n]
        rhs = jnp.concatenate([au[u], jnp.concatenate([jnp.zeros((n, pw), BF16), v[u]], axis=1)], axis=0)
        mc[u] = jnp.where(same_wide, _dot(lhs, rhs), 0.0)
    for u in units:
        c, p = u
        gam_row = jnp.sum(jnp.where(sub == first + c, gam_ref[...], 0.0), axis=0, keepdims=True)
        m_t = jnp.where(diag, gam_row[:, p * pw:(p + 1) * pw], 0.0) + mc[u][:, :pw]
        mr = _dot(jnp.concatenate([m_t, ry[u][:, :pw]], axis=0).astype(BF16), s_ref[p].astype(BF16))
        s_ref[p] = mr[:pw] + mc[u][:, pw:]
        y_ref[c * n:(c + 1) * n, p * pw:(p + 1) * pw] = mr[pw:] + ry[u][:, pw:]


def _k2(at, rt, v, btk, bgk, gam):
    t = at.shape[0]
    rows = RWKV_CHUNK * K2_CHUNKS
    per = K1_ROWS // rows
    seq = _row_spec(rows, D_RWKV)
    return pl.pallas_call(
        _k2_kernel,
        grid=(t // rows,),
        in_specs=[seq] * 3 + [pl.BlockSpec((D_RWKV, 4 * rows), lambda i: (0, i)),
                              pl.BlockSpec((D_RWKV, 2 * rows), lambda i: (0, i)),
                              pl.BlockSpec((None, 8, D_RWKV), lambda i: (i // per, 0, 0))],
        out_specs=seq,
        out_shape=jax.ShapeDtypeStruct((t, D_RWKV), F32),
        scratch_shapes=[pltpu.VMEM((N_HEADS // 2, 2 * HEAD, 2 * HEAD), F32)],
        compiler_params=_params("arbitrary"),
        name="k2_rwkv_chunked",
    )(at, rt, v, btk, bgk, gam)


def _tail(h, p_ref, nffn_ref, up_ref, down_ref, nple_ref, gate_ref, proj_ref):
    hn = _rms(h, nffn_ref[...]).astype(BF16)
    mlp = None
    for c in range(D_FF // FF_CHUNK):
        a = jnp.maximum(_dot(hn, up_ref[:, c * FF_CHUNK:(c + 1) * FF_CHUNK]), 0.0)
        part = _dot((a * a).astype(BF16), down_ref[c * FF_CHUNK:(c + 1) * FF_CHUNK, :])
        mlp = part if mlp is None else mlp + part
    h = h + mlp
    gate = _sigmoid(_dot(_rms(h, nple_ref[...]).astype(BF16), gate_ref[...]))
    return h + _dot(p_ref[...].astype(BF16), proj_ref[...]) * gate


def _k3_kernel(y_ref, g_ref, bv_ref, yb_ref, x_ref, p_ref, gnmean_ref, lnw_ref, lnb_ref, wout_ref,
               nffn_ref, up_ref, down_ref, nple_ref, gate_ref, proj_ref, nnext_ref, h_ref, hn_ref, perm_ref):
    y = y_ref[...]
    d = y - _dot(y.astype(BF16), gnmean_ref[...])
    var = _dot((d * d).astype(BF16), gnmean_ref[...])
    yn = d * lax.rsqrt(var + GN_EPS) * lnw_ref[...] + lnb_ref[...]
    ya = ((yn + bv_ref[...]) * g_ref[...]).astype(BF16)
    h = x_ref[...] + _dot(ya, wout_ref[0:D_RWKV, :]) + _dot(yb_ref[...], wout_ref[D_RWKV:, :])
    h = _tail(h, p_ref, nffn_ref, up_ref, down_ref, nple_ref, gate_ref, proj_ref)
    h_ref[...] = h
    hn = _rms(h, nnext_ref[...])
    blocks = hn.shape[0] // S5_BLOCK
    for c, lanes in enumerate(_lane_chunks(D_MODEL)):
        for m in range(blocks):
            perm_ref[c, PERM_PITCH * m:PERM_PITCH * m + S5_BLOCK, :] = hn[m * S5_BLOCK:(m + 1) * S5_BLOCK, lanes]
    for s in range(S5_BLOCK):
        for c, lanes in enumerate(_lane_chunks(D_MODEL)):
            hn_ref[s, :, lanes] = perm_ref[c, pl.ds(s, blocks, stride=PERM_PITCH), :].astype(BF16)


def _k3(y, g, bv, yb, x2d, p3d, ln_w, ln_b, wout, nffn, up, down, nple, gate, proj, nnext):
    t = x2d.shape[0]
    rows = min(TAIL_ROWS, t)
    consts = [_head_ones(1.0 / HEAD), ln_w, ln_b, wout, nffn, up, down, nple, gate, proj, nnext]
    return pl.pallas_call(
        _k3_kernel,
        grid=(t // rows,),
        in_specs=[_row_spec(rows, D_RWKV), _row_spec(rows, D_RWKV), _row_spec(rows, D_RWKV),
                  _row_spec(rows, D_CONV), _row_spec(rows, D_MODEL), _layer_spec(rows, 0)]
                 + [_const_spec(c.shape) for c in consts],
        out_specs=[_row_spec(rows, D_MODEL), _posmajor_spec(rows)],
        out_shape=[jax.ShapeDtypeStruct((t, D_MODEL), F32),
                   jax.ShapeDtypeStruct((S5_BLOCK, t // S5_BLOCK, D_MODEL), BF16)],
        scratch_shapes=[pltpu.VMEM((D_MODEL // 128, PERM_PITCH * (rows // S5_BLOCK), 128), F32)],
        compiler_params=_params("arbitrary"),
        name="k3_mix_out_mlp_ple",
    )(y, g, bv, yb, x2d, p3d, *consts)


def _shift_rows(x, s):
    n = x.shape[0]
    if s % 8 == 0:
        return jnp.concatenate([jnp.zeros((s, x.shape[1]), x.dtype), x[:n - s]], axis=0)
    keep = lax.broadcasted_iota(jnp.int32, x.shape, 0) >= s
    return jnp.where(keep, pltpu.roll(x, s, axis=0), 0.0)


def _k4_kernel(x_ref, c0_ref, pc_ref, qc_ref, are_ref, aim_ref, dsk_ref, y_ref, wt_ref, carry_ref):
    nb, rows, lanes = x_ref.shape
    gl = lanes // SSM_GROUP
    sw = 2 * SSM_STATE
    zero = jnp.zeros((), BF16)

    pw = 2 * SSM_GROUP
    npair = lanes // pw

    @pl.when(pl.program_id(1) == 0)
    def _():
        carry_ref[...] = jnp.zeros_like(carry_ref)
        c_row = lax.broadcasted_iota(jnp.int32, (gl * sw, lanes), 0) // sw
        c_col = lax.broadcasted_iota(jnp.int32, (gl * sw, lanes), 1) // SSM_GROUP
        cbd = jnp.where(c_row == c_col, jnp.concatenate([c0_ref[...]] * gl, axis=0), zero)
        b_row = lax.broadcasted_iota(jnp.int32, (lanes, gl * sw), 0) // SSM_GROUP
        b_col = lax.broadcasted_iota(jnp.int32, (lanes, gl * sw), 1) // sw
        for s in range(nb):
            lam_b = jnp.concatenate([pc_ref[j][s * pw:(s + 1) * pw, :] for j in range(npair)], axis=0)
            bbd = jnp.where(b_row == b_col, jnp.concatenate([lam_b] * gl, axis=1), zero)
            wt_ref[s * lanes:(s + 1) * lanes, :] = _dot(bbd, cbd).astype(BF16)

    slot = lax.broadcasted_iota(jnp.int32, (1, 128), 1) // pw

    def gather(src_of, src_slot_of, count):
        cols = []
        for w in range(count // 4):
            col = None
            for kk in range(4):
                src = src_of(4 * w + kk)
                shift = (pw * (kk - src_slot_of(4 * w + kk))) % 128
                src = pltpu.roll(src, shift, axis=1) if shift else src
                col = src if col is None else jnp.where(slot == kk, src, col)
            cols.append(col)
        return jnp.concatenate(cols, axis=1)

    ns = SSM_STATE
    p_same = ((lax.broadcasted_iota(jnp.int32, (nb * pw, 2 * ns), 0) // SSM_GROUP) % 2
              == lax.broadcasted_iota(jnp.int32, (nb * pw, 2 * ns), 1) // ns)
    parts_r, parts_i = [], []
    for j in range(npair):
        lhs = gather(lambda s: x_ref[s][:, 128 * (j // 4):128 * (j // 4 + 1)], lambda s: j % 4, nb)
        pcj = pc_ref[j]
        parts_r.append(_dot(lhs, jnp.where(p_same, jnp.concatenate([pcj[:, :ns]] * 2, axis=1), zero)))
        parts_i.append(_dot(lhs, jnp.where(p_same, jnp.concatenate([pcj[:, ns:]] * 2, axis=1), zero)))
    sr = jnp.concatenate(parts_r, axis=1)
    si = jnp.concatenate(parts_i, axis=1)

    first = lax.broadcasted_iota(jnp.int32, sr.shape, 0) == 0
    hr_in, hi_in = carry_ref[0:1, :], carry_ref[8:9, :]
    ar, ai = are_ref[0:1, :], aim_ref[0:1, :]
    sr = sr + jnp.where(first, ar * hr_in - ai * hi_in, 0.0)
    si = si + jnp.where(first, ar * hi_in + ai * hr_in, 0.0)
    for k in range(are_ref.shape[0]):
        if (1 << k) >= rows:
            break
        ar, ai = are_ref[k:k + 1, :], aim_ref[k:k + 1, :]
        tr, ti = _shift_rows(sr, 1 << k), _shift_rows(si, 1 << k)
        sr, si = sr + ar * tr - ai * ti, si + ar * ti + ai * tr
    carry_ref[0:1, :] = sr[rows - 1:rows, :]
    carry_ref[8:9, :] = si[rows - 1:rows, :]
    hbr = jnp.where(first, hr_in, _shift_rows(sr, 1)).astype(BF16)
    hbi = jnp.where(first, hi_in, _shift_rows(si, 1)).astype(BF16)

    y_in = []
    for j in range(npair):
        qcj = qc_ref[j]
        q_re = jnp.where(p_same, jnp.concatenate([qcj[:, :ns]] * 2, axis=1), zero)
        q_im = jnp.where(p_same, jnp.concatenate([qcj[:, ns:]] * 2, axis=1), zero)
        cols = slice(j * 2 * ns, (j + 1) * 2 * ns)
        y_in.append((_dot_nt(hbr[:, cols], q_re) + _dot_nt(hbi[:, cols], q_im)).astype(BF16))
    dsk = dsk_ref[...]
    for t in range(nb):
        xcat = jnp.concatenate([x_ref[s] for s in range(t + 1)], axis=1)
        y_t = gather(lambda j: y_in[j][:, 128 * (t // 4):128 * (t // 4 + 1)], lambda j: t % 4, npair)
        y = _dot(xcat, wt_ref[(nb - 1 - t) * lanes:, :]) + y_t.astype(F32)
        y_ref[t] = _gelu_tanh(y + dsk * x_ref[t].astype(F32)).astype(BF16)


def _k4(xs, c0, pc, qc, a_re, a_im, dsk):
    nb, m, d = xs.shape
    lanes = 256
    rows = min(S5_ROWS, m)
    return pl.pallas_call(
        _k4_kernel,
        grid=(d // lanes, m // rows),
        in_specs=[pl.BlockSpec((nb, rows, lanes), lambda q, i: (0, i, q)),
                  pl.BlockSpec((None,) + c0.shape[1:], lambda q, i: (q, 0, 0)),
                  pl.BlockSpec((None,) + pc.shape[1:], lambda q, i: (q, 0, 0, 0)),
                  pl.BlockSpec((None,) + qc.shape[1:], lambda q, i: (q, 0, 0, 0)),
                  pl.BlockSpec((None,) + a_re.shape[1:], lambda q, i: (q, 0, 0)),
                  pl.BlockSpec((None,) + a_im.shape[1:], lambda q, i: (q, 0, 0)),
                  pl.BlockSpec((1, lanes), lambda q, i: (0, q))],
        out_specs=pl.BlockSpec((nb, rows, lanes), lambda q, i: (0, i, q)),
        out_shape=jax.ShapeDtypeStruct((nb, m, d), BF16),
        scratch_shapes=[pltpu.VMEM((nb * lanes, lanes), BF16),
                        pltpu.VMEM((16, SSM_STATE * (lanes // SSM_GROUP)), F32)],
        compiler_params=_params("arbitrary", "arbitrary"),
        name="k4_s5",
    )(xs, c0, pc, qc, a_re, a_im, dsk)


def _s5_tables(lam_re, lam_im, log_step, b_re, b_im, c_re, c_im, d_skip, n_blocks):
    g, p, c, nb = SSM_GROUPS, SSM_STATE, SSM_GROUP, S5_BLOCK
    lre = jnp.minimum(lam_re.astype(F32), -1e-4)
    lim = lam_im.astype(F32)
    step = jnp.exp(log_step.astype(F32))[:, None]
    ar, ai = lre * step, lim * step
    n = jnp.arange(nb + 1, dtype=F32)[:, None, None]
    mag = jnp.exp(n * ar)
    pr, pi = mag * jnp.cos(n * ai), mag * jnp.sin(n * ai)
    nr, ni = pr[1] - 1.0, pi[1]
    den = lre * lre + lim * lim
    qr, qi = (nr * lre + ni * lim) / den, (ni * lre - nr * lim) / den
    bre, bim = b_re.astype(F32), b_im.astype(F32)
    bbr = qr[..., None] * bre - qi[..., None] * bim
    bbi = qr[..., None] * bim + qi[..., None] * bre
    cre, cim = c_re.astype(F32), c_im.astype(F32)
    tiles, pairs = g * c // 256, 256 // (2 * c)
    split = lambda m: m.reshape((tiles, pairs, 2) + m.shape[1:])
    both = lambda a, b: jnp.concatenate([a, b], axis=-1)
    bt_r, bt_i = bbr.transpose(0, 2, 1), bbi.transpose(0, 2, 1)
    b_for_re = split(both(bt_r, bt_i))[:, :, None]
    b_for_im = split(both(-bt_i, bt_r))[:, :, None]
    ps_r, ps_i = pr[nb - 1::-1].transpose(1, 0, 2), pi[nb - 1::-1].transpose(1, 0, 2)
    ps_r = split(both(ps_r, ps_r)).transpose(0, 1, 3, 2, 4)[..., None, :]
    ps_i = split(both(ps_i, ps_i)).transpose(0, 1, 3, 2, 4)[..., None, :]
    pc = (ps_r * b_for_re + ps_i * b_for_im).reshape(tiles, pairs, nb * 2 * c, 2 * p)
    c_for_re = split(both(cre, -cim))[:, :, None]
    c_for_im = split(both(-cim, -cre))[:, :, None]
    pt_r, pt_i = pr[1:].transpose(1, 0, 2), pi[1:].transpose(1, 0, 2)
    pt_r = split(both(pt_r, pt_r)).transpose(0, 1, 3, 2, 4)[..., None, :]
    pt_i = split(both(pt_i, pt_i)).transpose(0, 1, 3, 2, 4)[..., None, :]
    qc = (pt_r * c_for_re + pt_i * c_for_im).reshape(tiles, pairs, nb * 2 * c, 2 * p)
    ct_r, ct_i = cre.transpose(0, 2, 1), cim.transpose(0, 2, 1)
    c0 = jnp.stack([ct_r, -ct_i], axis=1).reshape(tiles, 256 // c, 2 * p, c)
    c0 = c0.transpose(0, 2, 1, 3).reshape(tiles, 2 * p, 256)
    levels = max(1, int(np.ceil(np.log2(max(n_blocks, 2)))))
    a_re, a_im = [pr[nb]], [pi[nb]]
    for _ in range(levels - 1):
        r, i = a_re[-1], a_im[-1]
        a_re.append(r * r - i * i)
        a_im.append(2.0 * r * i)
    a_re, a_im = jnp.stack(a_re, axis=0), jnp.stack(a_im, axis=0)
    a_re = a_re.reshape(levels, tiles, -1).transpose(1, 0, 2)
    a_im = a_im.reshape(levels, tiles, -1).transpose(1, 0, 2)
    return c0.astype(BF16), pc.astype(BF16), qc.astype(BF16), a_re, a_im, d_skip.astype(F32).reshape(1, g * c)


def _k5_kernel(ys_ref, h_ref, p_ref, w1_ref, w2_ref, nffn_ref, up_ref, down_ref, nple_ref, gate_ref,
               proj_ref, nfinal_ref, o_ref, perm_ref):
    blocks = ys_ref.shape[1]
    for s in range(S5_BLOCK):
        for c, lanes in enumerate(_lane_chunks(D_MODEL)):
            perm_ref[c, pl.ds(s, blocks, stride=PERM_PITCH), :] = ys_ref[s, :, lanes].astype(F32)
    yg = jnp.concatenate(
        [jnp.concatenate([perm_ref[c, PERM_PITCH * m:PERM_PITCH * m + S5_BLOCK, :] for m in range(blocks)], axis=0)
         for c in range(D_MODEL // 128)], axis=1).astype(BF16)
    h = h_ref[...] + _dot(yg, w1_ref[...]) * _sigmoid(_dot(yg, w2_ref[...]))
    h = _tail(h, p_ref, nffn_ref, up_ref, down_ref, nple_ref, gate_ref, proj_ref)
    o_ref[...] = _rms(h, nfinal_ref[...])


def _k5(ys, h, p3d, w1, w2, nffn, up, down, nple, gate, proj, nfinal):
    t = h.shape[0]
    rows = min(TAIL_ROWS, t)
    consts = [w1, w2, nffn, up, down, nple, gate, proj, nfinal]
    return pl.pallas_call(
        _k5_kernel,
        grid=(t // rows,),
        in_specs=[_posmajor_spec(rows), _row_spec(rows, D_MODEL), _layer_spec(rows, 1)]
                 + [_const_spec(c.shape) for c in consts],
        out_specs=_row_spec(rows, D_MODEL),
        out_shape=jax.ShapeDtypeStruct((t, D_MODEL), F32),
        scratch_shapes=[pltpu.VMEM((D_MODEL // 128, PERM_PITCH * (rows // S5_BLOCK), 128), F32)],
        compiler_params=_params("arbitrary"),
        name="k5_glu_mlp_ple_norm",
    )(ys, h, p3d, *consts)


def _row(vec):
    return vec.astype(F32).reshape(1, -1)


def _lora_rows(m, first, width):
    return jnp.zeros((D_LORA, width), F32).at[first:first + m.shape[0]].set(m.astype(F32)).astype(BF16)


def kernel(x, p, l0_norm_mix, l0_w_in, l0_shift_mu, l0_w0, l0_w_lora_up, l0_a0, l0_a_lora_up, l0_g_lora_up, l0_k_k, l0_k_a, l0_r_k, l0_ln_w, l0_ln_b, l0_conv_w, l0_w_out, l0_norm_ffn, l0_ffn_up, l0_ffn_down, l0_norm_ple, l0_ple_gate, l0_ple_proj, l1_norm_mix, l1_lambda_re, l1_lambda_im, l1_log_step, l1_b_re, l1_b_im, l1_c_re, l1_c_im, l1_d_skip, l1_glu_w1, l1_glu_w2, l1_norm_ffn, l1_ffn_up, l1_ffn_down, l1_norm_ple, l1_ple_gate, l1_ple_proj, norm_final):
    bsz, t, _ = x.shape
    assert bsz == 1 and t % K1_ROWS == 0 and t % TAIL_ROWS == 0
    x2d = x.reshape(t, D_MODEL)
    p3d = p.reshape(p.shape[0], t, D_PLE)

    at, rt, v, btk, bgk, gam, g, bv, yb = _k1(
        x2d, _row(l0_norm_mix), l0_w_in.astype(BF16), _row(l0_shift_mu[:D_IN_A]), _row(l0_w0),
        _lora_rows(l0_w_lora_up, 0, D_RWKV), _row(l0_a0), _lora_rows(l0_a_lora_up, LORA_W, D_RWKV),
        _lora_rows(l0_g_lora_up, LORA_W + LORA_A, D_RWKV), _row(l0_k_k), _row(l0_k_a), _row(l0_r_k),
        l0_conv_w.astype(F32))
    y = _k2(at, rt, v, btk, bgk, gam)
    h, hn = _k3(y, g, bv, yb, x2d, p3d, _row(l0_ln_w), _row(l0_ln_b), l0_w_out.astype(BF16),
                _row(l0_norm_ffn), l0_ffn_up.astype(BF16), l0_ffn_down.astype(BF16),
                _row(l0_norm_ple), l0_ple_gate.astype(BF16), l0_ple_proj.astype(BF16), _row(l1_norm_mix))

    tables = _s5_tables(l1_lambda_re, l1_lambda_im, l1_log_step, l1_b_re, l1_b_im, l1_c_re, l1_c_im,
                        l1_d_skip, t // S5_BLOCK)
    ys = _k4(hn, *tables)

    out = _k5(ys, h, p3d, l1_glu_w1.astype(BF16), l1_glu_w2.astype(BF16), _row(l1_norm_ffn),
              l1_ffn_up.astype(BF16), l1_ffn_down.astype(BF16), _row(l1_norm_ple),
              l1_ple_gate.astype(BF16), l1_ple_proj.astype(BF16), _row(norm_final))
    return out.reshape(bsz, t, D_MODEL)
```

```python
import numpy as np
import jax
import jax.numpy as jnp
from jax import lax
from jax.experimental import pallas as pl
from jax.experimental.pallas import tpu as pltpu

F32 = jnp.float32
BF16 = jnp.bfloat16

D_MODEL = 1024
N_HEADS = 8
HEAD = 64
D_RWKV = N_HEADS * HEAD
D_CONV = D_MODEL - D_RWKV
LORA_W, LORA_A, LORA_G = 64, 64, 128
D_LORA = LORA_W + LORA_A + LORA_G
D_IN_A = 3 * D_RWKV + D_LORA
D_IN = D_IN_A + 3 * D_CONV
GN_EPS = 64e-5
RMS_EPS = 1e-6
SSM_GROUPS, SSM_GROUP, SSM_STATE = 64, 16, 64
D_FF = 4 * D_MODEL
D_PLE = 256

VMEM_LIMIT = 60 * 1024 * 1024

RWKV_CHUNK = 64
K1_ROWS = 512
K1_CHUNKS = K1_ROWS // RWKV_CHUNK
K2_CHUNKS = 8
TAIL_ROWS = 512
FF_CHUNK = 512
S5_BLOCK = 16
S5_ROWS = 512
PERM_PITCH = S5_BLOCK + 1


def _dot(a, b):
    return jnp.dot(a, b, preferred_element_type=F32)


def _dot_nt(a, b):
    return lax.dot_general(a, b, (((1,), (1,)), ((), ())), preferred_element_type=F32)


def _split_bf16(x, n):
    pieces = []
    for _ in range(n - 1):
        p = x.astype(BF16)
        pieces.append(p)
        x = x - p.astype(F32)
    pieces.append(x.astype(BF16))
    return pieces


def _rms(x, gain):
    return x * lax.rsqrt(jnp.mean(x * x, axis=-1, keepdims=True) + RMS_EPS) * gain


def _sigmoid(x):
    return 1.0 / (1.0 + jnp.exp(-x))


def _softplus(x):
    return jnp.maximum(x, 0.0) + jnp.log(1.0 + jnp.exp(-jnp.abs(x)))


def _gelu_tanh(x):
    return 0.5 * x * (1.0 + jnp.tanh(np.sqrt(2.0 / np.pi).astype(np.float32) * (x + 0.044715 * (x * x * x))))


def _const_spec(shape):
    nd = len(shape)
    return pl.BlockSpec(shape, lambda *_: (0,) * nd, pipeline_mode=pl.Buffered(1))


def _row_spec(rows, cols):
    return pl.BlockSpec((rows, cols), lambda i: (i, 0))


def _layer_spec(rows, layer):
    return pl.BlockSpec((None, rows, D_PLE), lambda i: (layer, i, 0))


def _posmajor_spec(rows):
    return pl.BlockSpec((S5_BLOCK, rows // S5_BLOCK, D_MODEL), lambda i: (0, i, 0))


def _lane_chunks(width):
    return [slice(c, c + 128) for c in range(0, width, 128)]


def _params(*sem):
    return pltpu.CompilerParams(dimension_semantics=sem, vmem_limit_bytes=VMEM_LIMIT)


def _head_ones(scale):
    lane = np.arange(D_RWKV)
    return jnp.asarray(scale * ((lane[:, None] // HEAD) == (lane[None, :] // HEAD)), BF16)


def _k1_kernel(x_ref, gmix_ref, win_ref, mu_ref, w0_ref, ww_ref, a0_ref, wa_ref, wg_ref, kk_ref, ka_ref,
               rk_ref, cw_ref, tri_ref, sel_ref, hsum_ref,
               at_ref, rt_ref, v_ref, btk_ref, bgk_ref, gam_ref, g_ref, bv_ref, yb_ref,
               zs_ref, us_ref):
    rows = x_ref.shape[0]

    @pl.when(pl.program_id(0) == 0)
    def _():
        zs_ref[0:8, :] = jnp.zeros((8, D_IN_A), F32)
        us_ref[0:8, :] = jnp.zeros((8, D_CONV), F32)

    xn = _rms(x_ref[...], gmix_ref[...])
    z = _dot(xn.astype(BF16), win_ref[...])

    za = z[:, :D_IN_A]
    zs_ref[8:8 + rows, :] = za
    za_prev = zs_ref[7:7 + rows, :]
    zs_ref[7:8, :] = za[rows - 1:rows, :]
    za = za + mu_ref[...] * (za_prev - za)

    r = za[:, 0:D_RWKV]
    k = za[:, D_RWKV:2 * D_RWKV]
    v = za[:, 2 * D_RWKV:3 * D_RWKV]
    lora_in = za[:, 3 * D_RWKV:]

    w_log = -_softplus(-(w0_ref[...] + _dot(jnp.tanh(lora_in).astype(BF16), ww_ref[...]))) - 0.5
    logw = -jnp.exp(w_log)
    lr = _sigmoid(a0_ref[...] + _dot(lora_in.astype(BF16), wa_ref[...]))
    g_ref[...] = _dot(_sigmoid(lora_in).astype(BF16), wg_ref[...])

    lw = _split_bf16(logw, 3)
    c_incl = sum(_dot(tri_ref[...], p) for p in lw[:2])
    c_chunk = sum(_dot(sel_ref[...], p) for p in lw)
    gam_ref[...] = jnp.exp(c_chunk)
    chunk_id = lax.broadcasted_iota(jnp.int32, (rows, 1), 0) // RWKV_CHUNK
    c_tot = c_chunk[0:1, :]
    for i in range(1, rows // RWKV_CHUNK):
        c_tot = jnp.where(chunk_id == i, c_chunk[i:i + 1, :], c_tot)
    e_neg = jnp.exp(-c_incl)
    e_rest = jnp.exp(c_tot - c_incl)

    kk = k * kk_ref[...]
    kk = kk / jnp.maximum(jnp.sqrt(_dot((kk * kk).astype(BF16), hsum_ref[...])), 1e-12)
    kmod = k * (1.0 + (lr - 1.0) * ka_ref[...])
    at_ref[...] = (-kk * jnp.exp(c_incl - logw)).astype(BF16)
    rt_ref[...] = (r * jnp.exp(c_incl)).astype(BF16)

    def chunk_major_t(parts):
        n = RWKV_CHUNK
        stacked = [p[c * n:(c + 1) * n] for c in range(rows // n) for p in parts]
        return jnp.concatenate(stacked, axis=0).T.astype(BF16)

    b_t, k_t = kk * lr * e_neg, kmod * e_neg
    btk_ref[...] = chunk_major_t([b_t, b_t, k_t, k_t])
    bgk_ref[...] = chunk_major_t([kk * lr * e_rest, kmod * e_rest])
    v_ref[...] = v.astype(BF16)
    bv_ref[...] = _dot((r * kmod * rk_ref[...]).astype(BF16), hsum_ref[...]) * v

    zb = z[:, D_IN_A:]
    b_gate = zb[:, :D_CONV]
    u = zb[:, D_CONV:2 * D_CONV] * zb[:, 2 * D_CONV:]
    us_ref[8:8 + rows, :] = u
    u1 = us_ref[7:7 + rows, :]
    u2 = us_ref[6:6 + rows, :]
    us_ref[6:8, :] = u[rows - 2:rows, :]
    cw = cw_ref[...]
    yb_ref[...] = (b_gate * (cw[0:1] * u + cw[1:2] * u1 + cw[2:3] * u2)).astype(BF16)


def _k1(x2d, gmix, win, mu, w0, ww, a0, wa, wg, k_k, k_a, r_k, conv_w):
    t = x2d.shape[0]
    rows = K1_ROWS
    r_idx = np.arange(rows)
    same = (r_idx[:, None] // RWKV_CHUNK) == (r_idx[None, :] // RWKV_CHUNK)
    tri = jnp.asarray(same & (r_idx[None, :] <= r_idx[:, None]), BF16)
    sel = jnp.asarray(np.arange(K1_CHUNKS)[:, None] == (r_idx[None, :] // RWKV_CHUNK), BF16)

    consts = [gmix, win, mu, w0, ww, a0, wa, wg, k_k, k_a, r_k, conv_w, tri, sel, _head_ones(1.0)]
    wide = jax.ShapeDtypeStruct((t, D_RWKV), BF16)
    out_shape = [wide] * 3 + [
        jax.ShapeDtypeStruct((D_RWKV, 4 * t), BF16),
        jax.ShapeDtypeStruct((D_RWKV, 2 * t), BF16),
        jax.ShapeDtypeStruct((t // rows, K1_CHUNKS, D_RWKV), F32),
        jax.ShapeDtypeStruct((t, D_RWKV), F32),
        jax.ShapeDtypeStruct((t, D_RWKV), F32),
        jax.ShapeDtypeStruct((t, D_CONV), BF16),
    ]
    out_specs = [_row_spec(rows, D_RWKV)] * 3 + [
        pl.BlockSpec((D_RWKV, 4 * rows), lambda i: (0, i)), pl.BlockSpec((D_RWKV, 2 * rows), lambda i: (0, i)),
        pl.BlockSpec((None, K1_CHUNKS, D_RWKV), lambda i: (i, 0, 0)),
        _row_spec(rows, D_RWKV), _row_spec(rows, D_RWKV), _row_spec(rows, D_CONV)]
    return pl.pallas_call(
        _k1_kernel,
        grid=(t // rows,),
        in_specs=[_row_spec(rows, D_MODEL)] + [_const_spec(c.shape) for c in consts],
        out_specs=out_specs,
        out_shape=out_shape,
        scratch_shapes=[pltpu.VMEM((rows + 8, D_IN_A), F32), pltpu.VMEM((rows + 8, D_CONV), F32)],
        compiler_params=_params("arbitrary"),
        name="k1_inproj_prep",
    )(x2d, *consts)


def _k2_kernel(at_ref, rt_ref, v_ref, btk_ref, bgk_ref, gam_ref, y_ref, s_ref):
    n = RWKV_CHUNK
    pw = 2 * HEAD
    step = pl.program_id(0)

    @pl.when(step == 0)
    def _():
        s_ref[...] = jnp.zeros_like(s_ref)

    zero = jnp.zeros((), BF16)
    iota = lambda shape, d: lax.broadcasted_iota(jnp.int32, shape, d)
    same = iota((pw, pw), 0) // HEAD == iota((pw, pw), 1) // HEAD
    same_wide = jnp.concatenate([same, same], axis=1)
    diag = iota((pw, pw), 0) == iota((pw, pw), 1)
    eye2 = jnp.where(iota((n, pw), 0) == iota((n, pw), 1) % n, 1.0, 0.0).astype(F32)
    row2, col2 = iota((2 * n, 2 * pw), 0), iota((2 * n, 2 * pw), 1) % n
    tri2 = col2 <= jnp.where(row2 < n, row2 - 1, row2 - n)
    sub = iota((K1_CHUNKS, D_RWKV), 0)
    first = (step * K2_CHUNKS) % (K1_ROWS // n)

    def bd(m):
        return jnp.where(same, jnp.concatenate([m, m], axis=0), zero)

    units = [(c, p) for c in range(K2_CHUNKS) for p in range(N_HEADS // 2)]
    tile = lambda ref, u: ref[u[0] * n:(u[0] + 1) * n, u[1] * pw:(u[1] + 1) * pw]
    a = {u: tile(at_ref, u) for u in units}
    r = {u: tile(rt_ref, u) for u in units}
    v = {u: tile(v_ref, u) for u in units}
    gm = {}
    for u in units:
        c, p = u
        rhs = jnp.where(same_wide, btk_ref[p * pw:(p + 1) * pw, c * 4 * n:(c + 1) * 4 * n], zero)
        gm[u] = jnp.where(tri2, _dot(jnp.concatenate([a[u], r[u]], axis=0), rhs), 0.0)
    gmb = {u: gm[u].astype(BF16) for u in units}
    wv = {u: _dot(gmb[u][:, pw:], bd(v[u])) for u in units}

    q = {u: gm[u][:n, :pw] for u in units}
    tinv = {u: eye2 + q[u] for u in units}
    for u in units:
        qb = q[u].astype(BF16)
        q[u] = _dot(qb, bd(qb))
    for level in range(1, 6):
        for u in units:
            qb = q[u].astype(BF16)
            if level < 5:
                tq = _dot(jnp.concatenate([tinv[u].astype(BF16), qb], axis=0), bd(qb))
                tinv[u], q[u] = tinv[u] + tq[:n], tq[n:]
            else:
                tinv[u] = tinv[u] + _dot(tinv[u].astype(BF16), bd(qb))

    au, ry, mc = {}, {}, {}
    for u in units:
        rhs = jnp.concatenate([bd(a[u]), bd(wv[u][:n].astype(BF16))], axis=1)
        au[u] = _dot(tinv[u].astype(BF16), rhs).astype(BF16)
    for u in units:
        rhs = jnp.concatenate([bd(au[u][:, :pw]), bd(au[u][:, pw:])], axis=1)
        ry[u] = jnp.concatenate([r[u].astype(F32), wv[u][n:]], axis=1) + _dot(gmb[u][n:, :pw], rhs)
    for u in units:
        c, p = u
        lhs = bgk_ref[p * pw:(p + 1) * pw, c * 2 * n:(c + 1) * 2 * n]
        rhs = jnp.concatenate([au[u], jnp.concatenate([jnp.zeros((n, pw), BF16), v[u]], axis=1)], axis=0)
        mc[u] = jnp.where(same_wide, _dot(lhs, rhs), 0.0)
    for u in units:
        c, p = u
        gam_row = jnp.sum(jnp.where(sub == first + c, gam_ref[...], 0.0), axis=0, keepdims=True)
        m_t = jnp.where(diag, gam_row[:, p * pw:(p + 1) * pw], 0.0) + mc[u][:, :pw]
        mr = _dot(jnp.concatenate([m_t, ry[u][:, :pw]], axis=0).astype(BF16), s_ref[p].astype(BF16))
        s_ref[p] = mr[:pw] + mc[u][:, pw:]
        y_ref[c * n:(c + 1) * n, p * pw:(p + 1) * pw] = mr[pw:] + ry[u][:, pw:]


def _k2(at, rt, v, btk, bgk, gam):
    t = at.shape[0]
    rows = RWKV_CHUNK * K2_CHUNKS
    per = K1_ROWS // rows
    seq = _row_spec(rows, D_RWKV)
    return pl.pallas_call(
        _k2_kernel,
        grid=(t // rows,),
        in_specs=[seq] * 3 + [pl.BlockSpec((D_RWKV, 4 * rows), lambda i: (0, i)),
                              pl.BlockSpec((D_RWKV, 2 * rows), lambda i: (0, i)),
                              pl.BlockSpec((None, K1_CHUNKS, D_RWKV), lambda i: (i // per, 0, 0))],
        out_specs=seq,
        out_shape=jax.ShapeDtypeStruct((t, D_RWKV), F32),
        scratch_shapes=[pltpu.VMEM((N_HEADS // 2, 2 * HEAD, 2 * HEAD), F32)],
        compiler_params=_params("arbitrary"),
        name="k2_rwkv_chunked",
    )(at, rt, v, btk, bgk, gam)


_HBM_SPEC = pl.BlockSpec(memory_space=pltpu.HBM)


def _ffn_scratch():
    return [pltpu.VMEM((D_MODEL, D_FF), BF16), pltpu.VMEM((D_FF, D_MODEL), BF16),
            pltpu.VMEM((2, D_MODEL, FF_CHUNK), F32), pltpu.VMEM((2, FF_CHUNK, D_MODEL), F32),
            pltpu.SemaphoreType.DMA((2, 2))]


def _ffn_copies(c, ffn):
    up_hbm, down_hbm, _, _, stage_up, stage_down, sem = ffn
    cols = pl.ds(c * FF_CHUNK, FF_CHUNK)
    return (pltpu.make_async_copy(up_hbm.at[:, cols], stage_up.at[c % 2], sem.at[0, c % 2]),
            pltpu.make_async_copy(down_hbm.at[cols, :], stage_down.at[c % 2], sem.at[1, c % 2]))


def _ffn_prefetch(first_step, ffn):
    @pl.when(first_step)
    def _():
        for c in range(min(2, D_FF // FF_CHUNK)):
            for copy in _ffn_copies(c, ffn):
                copy.start()


def _tail(h, first_step, p_ref, nffn_ref, ffn, nple_ref, gate_ref, proj_ref):
    _, _, up_ref, down_ref, stage_up, stage_down, _ = ffn
    chunks = D_FF // FF_CHUNK
    hn = _rms(h, nffn_ref[...]).astype(BF16)
    mlp = None
    for c in range(chunks):
        cols = slice(c * FF_CHUNK, (c + 1) * FF_CHUNK)

        @pl.when(first_step)
        def _(c=c, cols=cols):
            for copy in _ffn_copies(c, ffn):
                copy.wait()
            up_ref[:, cols] = stage_up[c % 2].astype(BF16)
            down_ref[cols, :] = stage_down[c % 2].astype(BF16)
            if c + 2 < chunks:
                for copy in _ffn_copies(c + 2, ffn):
                    copy.start()

        a = jnp.maximum(_dot(hn, up_ref[:, cols]), 0.0)
        part = _dot((a * a).astype(BF16), down_ref[cols, :])
        mlp = part if mlp is None else mlp + part
    h = h + mlp
    gate = _sigmoid(_dot(_rms(h, nple_ref[...]).astype(BF16), gate_ref[...]))
    return h + _dot(p_ref[...].astype(BF16), proj_ref[...]) * gate


def _k3_kernel(y_ref, g_ref, bv_ref, yb_ref, x_ref, p_ref, gnmean_ref, lnw_ref, lnb_ref, wout_ref,
               nffn_ref, up_hbm, down_hbm, nple_ref, gate_ref, proj_ref, nnext_ref, h_ref, hn_ref, perm_ref,
               *ffn_scratch):
    first_step = pl.program_id(0) == 0
    ffn = (up_hbm, down_hbm) + ffn_scratch
    _ffn_prefetch(first_step, ffn)
    y = y_ref[...]
    d = y - _dot(y.astype(BF16), gnmean_ref[...])
    var = _dot((d * d).astype(BF16), gnmean_ref[...])
    yn = d * lax.rsqrt(var + GN_EPS) * lnw_ref[...] + lnb_ref[...]
    ya = ((yn + bv_ref[...]) * g_ref[...]).astype(BF16)
    h = x_ref[...] + _dot(ya, wout_ref[0:D_RWKV, :]) + _dot(yb_ref[...], wout_ref[D_RWKV:, :])
    h = _tail(h, first_step, p_ref, nffn_ref, ffn, nple_ref, gate_ref, proj_ref)
    h_ref[...] = h
    hn = _rms(h, nnext_ref[...])
    blocks = hn.shape[0] // S5_BLOCK
    for c, lanes in enumerate(_lane_chunks(D_MODEL)):
        for m in range(blocks):
            perm_ref[c, PERM_PITCH * m:PERM_PITCH * m + S5_BLOCK, :] = hn[m * S5_BLOCK:(m + 1) * S5_BLOCK, lanes]
    for s in range(S5_BLOCK):
        for c, lanes in enumerate(_lane_chunks(D_MODEL)):
            hn_ref[s, :, lanes] = perm_ref[c, pl.ds(s, blocks, stride=PERM_PITCH), :].astype(BF16)


def _k3(y, g, bv, yb, x2d, p3d, ln_w, ln_b, wout, nffn, up, down, nple, gate, proj, nnext):
    t = x2d.shape[0]
    rows = min(TAIL_ROWS, t)
    consts = [_head_ones(1.0 / HEAD), ln_w, ln_b, wout, nffn, up, down, nple, gate, proj, nnext]
    return pl.pallas_call(
        _k3_kernel,
        grid=(t // rows,),
        in_specs=[_row_spec(rows, D_RWKV), _row_spec(rows, D_RWKV), _row_spec(rows, D_RWKV),
                  _row_spec(rows, D_CONV), _row_spec(rows, D_MODEL), _layer_spec(rows, 0)]
                 + [_HBM_SPEC if c is up or c is down else _const_spec(c.shape) for c in consts],
        out_specs=[_row_spec(rows, D_MODEL), _posmajor_spec(rows)],
        out_shape=[jax.ShapeDtypeStruct((t, D_MODEL), F32),
                   jax.ShapeDtypeStruct((S5_BLOCK, t // S5_BLOCK, D_MODEL), BF16)],
        scratch_shapes=[pltpu.VMEM((D_MODEL // 128, PERM_PITCH * (rows // S5_BLOCK), 128), F32)] + _ffn_scratch(),
        compiler_params=_params("arbitrary"),
        name="k3_mix_out_mlp_ple",
    )(y, g, bv, yb, x2d, p3d, *consts)


def _shift_rows(x, s):
    n = x.shape[0]
    if s % 8 == 0:
        return jnp.concatenate([jnp.zeros((s, x.shape[1]), x.dtype), x[:n - s]], axis=0)
    keep = lax.broadcasted_iota(jnp.int32, x.shape, 0) >= s
    return jnp.where(keep, pltpu.roll(x, s, axis=0), 0.0)


def _k4_kernel(x_ref, c0_ref, pc_ref, qc_ref, are_ref, aim_ref, dsk_ref, y_ref, wt_ref, carry_ref):
    nb, rows, lanes = x_ref.shape
    sw = 2 * SSM_STATE
    zero = jnp.zeros((), BF16)

    pw = 2 * SSM_GROUP
    npair = lanes // pw

    @pl.when(pl.program_id(1) == 0)
    def _():
        carry_ref[...] = jnp.zeros_like(carry_ref)
        b_same = ((lax.broadcasted_iota(jnp.int32, (nb * pw, 2 * sw), 0) // SSM_GROUP) % 2
                  == lax.broadcasted_iota(jnp.int32, (nb * pw, 2 * sw), 1) // sw)
        c_group = lax.broadcasted_iota(jnp.int32, (2 * sw, lanes), 1) // SSM_GROUP
        c_half = lax.broadcasted_iota(jnp.int32, (2 * sw, lanes), 0) // sw
        c_two = jnp.concatenate([c0_ref[...]] * 2, axis=0)
        for j in range(npair):
            lam_b = jnp.where(b_same, jnp.concatenate([pc_ref[j]] * 2, axis=1), zero)
            taps = _dot(lam_b, jnp.where(c_group == 2 * j + c_half, c_two, zero)).astype(BF16)
            for s in range(nb):
                wt_ref[s * lanes + j * pw:s * lanes + (j + 1) * pw, :] = taps[s * pw:(s + 1) * pw, :]

    slot = lax.broadcasted_iota(jnp.int32, (1, 128), 1) // pw

    def gather(src_of, src_slot_of, count):
        cols = []
        for w in range(count // 4):
            col = None
            for kk in range(4):
                src = src_of(4 * w + kk)
                shift = (pw * (kk - src_slot_of(4 * w + kk))) % 128
                src = pltpu.roll(src, shift, axis=1) if shift else src
                col = src if col is None else jnp.where(slot == kk, src, col)
            cols.append(col)
        return jnp.concatenate(cols, axis=1)

    ns = SSM_STATE
    p_same = ((lax.broadcasted_iota(jnp.int32, (nb * pw, 2 * ns), 0) // SSM_GROUP) % 2
              == lax.broadcasted_iota(jnp.int32, (nb * pw, 2 * ns), 1) // ns)
    parts_r, parts_i = [], []
    for j in range(npair):
        lhs = gather(lambda s: x_ref[s][:, 128 * (j // 4):128 * (j // 4 + 1)], lambda s: j % 4, nb)
        pcj = pc_ref[j]
        parts_r.append(_dot(lhs, jnp.where(p_same, jnp.concatenate([pcj[:, :ns]] * 2, axis=1), zero)))
        parts_i.append(_dot(lhs, jnp.where(p_same, jnp.concatenate([pcj[:, ns:]] * 2, axis=1), zero)))
    sr = jnp.concatenate(parts_r, axis=1)
    si = jnp.concatenate(parts_i, axis=1)

    first = lax.broadcasted_iota(jnp.int32, sr.shape, 0) == 0
    hr_in, hi_in = carry_ref[0:1, :], carry_ref[8:9, :]
    ar, ai = are_ref[0:1, :], aim_ref[0:1, :]
    sr = sr + jnp.where(first, ar * hr_in - ai * hi_in, 0.0)
    si = si + jnp.where(first, ar * hi_in + ai * hr_in, 0.0)
    for k in range(are_ref.shape[0]):
        if (1 << k) >= rows:
            break
        ar, ai = are_ref[k:k + 1, :], aim_ref[k:k + 1, :]
        tr, ti = _shift_rows(sr, 1 << k), _shift_rows(si, 1 << k)
        sr, si = sr + ar * tr - ai * ti, si + ar * ti + ai * tr
    carry_ref[0:1, :] = sr[rows - 1:rows, :]
    carry_ref[8:9, :] = si[rows - 1:rows, :]
    hbr = jnp.where(first, hr_in, _shift_rows(sr, 1)).astype(BF16)
    hbi = jnp.where(first, hi_in, _shift_rows(si, 1)).astype(BF16)

    y_in = []
    for j in range(npair):
        qcj = qc_ref[j]
        q_re = jnp.where(p_same, jnp.concatenate([qcj[:, :ns]] * 2, axis=1), zero)
        q_im = jnp.where(p_same, jnp.concatenate([qcj[:, ns:]] * 2, axis=1), zero)
        cols = slice(j * 2 * ns, (j + 1) * 2 * ns)
        y_in.append((_dot_nt(hbr[:, cols], q_re) + _dot_nt(hbi[:, cols], q_im)).astype(BF16))
    dsk = dsk_ref[...]
    for t in range(nb):
        xcat = jnp.concatenate([x_ref[s] for s in range(t + 1)], axis=1)
        y_t = gather(lambda j: y_in[j][:, 128 * (t // 4):128 * (t // 4 + 1)], lambda j: t % 4, npair)
        y = _dot(xcat, wt_ref[(nb - 1 - t) * lanes:, :]) + y_t.astype(F32)
        y_ref[t] = _gelu_tanh(y + dsk * x_ref[t].astype(F32)).astype(BF16)


def _k4(xs, c0, pc, qc, a_re, a_im, dsk):
    nb, m, d = xs.shape
    lanes = 256
    rows = min(S5_ROWS, m)
    return pl.pallas_call(
        _k4_kernel,
        grid=(d // lanes, m // rows),
        in_specs=[pl.BlockSpec((nb, rows, lanes), lambda q, i: (0, i, q)),
                  pl.BlockSpec((None,) + c0.shape[1:], lambda q, i: (q, 0, 0)),
                  pl.BlockSpec((None,) + pc.shape[1:], lambda q, i: (q, 0, 0, 0)),
                  pl.BlockSpec((None,) + qc.shape[1:], lambda q, i: (q, 0, 0, 0)),
                  pl.BlockSpec((None,) + a_re.shape[1:], lambda q, i: (q, 0, 0)),
                  pl.BlockSpec((None,) + a_im.shape[1:], lambda q, i: (q, 0, 0)),
                  pl.BlockSpec((1, lanes), lambda q, i: (0, q))],
        out_specs=pl.BlockSpec((nb, rows, lanes), lambda q, i: (0, i, q)),
        out_shape=jax.ShapeDtypeStruct((nb, m, d), BF16),
        scratch_shapes=[pltpu.VMEM((nb * lanes, lanes), BF16),
                        pltpu.VMEM((16, SSM_STATE * (lanes // SSM_GROUP)), F32)],
        compiler_params=_params("arbitrary", "arbitrary"),
        name="k4_s5",
    )(xs, c0, pc, qc, a_re, a_im, dsk)


def _s5_tables(lam_re, lam_im, log_step, b_re, b_im, c_re, c_im, d_skip, n_blocks):
    g, p, c, nb = SSM_GROUPS, SSM_STATE, SSM_GROUP, S5_BLOCK
    lre = jnp.minimum(lam_re.astype(F32), -1e-4)
    lim = lam_im.astype(F32)
    step = jnp.exp(log_step.astype(F32))[:, None]
    ar, ai = lre * step, lim * step
    n = jnp.arange(nb + 1, dtype=F32)[:, None, None]
    mag = jnp.exp(n * ar)
    pr, pi = mag * jnp.cos(n * ai), mag * jnp.sin(n * ai)
    nr, ni = pr[1] - 1.0, pi[1]
    den = lre * lre + lim * lim
    qr, qi = (nr * lre + ni * lim) / den, (ni * lre - nr * lim) / den
    bre, bim = b_re.astype(F32), b_im.astype(F32)
    bbr = qr[..., None] * bre - qi[..., None] * bim
    bbi = qr[..., None] * bim + qi[..., None] * bre
    cre, cim = c_re.astype(F32), c_im.astype(F32)
    tiles, pairs = g * c // 256, 256 // (2 * c)
    split = lambda m: m.reshape((tiles, pairs, 2) + m.shape[1:])
    both = lambda a, b: jnp.concatenate([a, b], axis=-1)
    bt_r, bt_i = bbr.transpose(0, 2, 1), bbi.transpose(0, 2, 1)
    b_for_re = split(both(bt_r, bt_i))[:, :, None]
    b_for_im = split(both(-bt_i, bt_r))[:, :, None]
    ps_r, ps_i = pr[nb - 1::-1].transpose(1, 0, 2), pi[nb - 1::-1].transpose(1, 0, 2)
    ps_r = split(both(ps_r, ps_r)).transpose(0, 1, 3, 2, 4)[..., None, :]
    ps_i = split(both(ps_i, ps_i)).transpose(0, 1, 3, 2, 4)[..., None, :]
    pc = (ps_r * b_for_re + ps_i * b_for_im).reshape(tiles, pairs, nb * 2 * c, 2 * p)
    c_for_re = split(both(cre, -cim))[:, :, None]
    c_for_im = split(both(-cim, -cre))[:, :, None]
    pt_r, pt_i = pr[1:].transpose(1, 0, 2), pi[1:].transpose(1, 0, 2)
    pt_r = split(both(pt_r, pt_r)).transpose(0, 1, 3, 2, 4)[..., None, :]
    pt_i = split(both(pt_i, pt_i)).transpose(0, 1, 3, 2, 4)[..., None, :]
    qc = (pt_r * c_for_re + pt_i * c_for_im).reshape(tiles, pairs, nb * 2 * c, 2 * p)
    ct_r, ct_i = cre.transpose(0, 2, 1), cim.transpose(0, 2, 1)
    c0 = jnp.stack([ct_r, -ct_i], axis=1).reshape(tiles, 256 // c, 2 * p, c)
    c0 = c0.transpose(0, 2, 1, 3).reshape(tiles, 2 * p, 256)
    levels = max(1, int(np.ceil(np.log2(max(n_blocks, 2)))))
    a_re, a_im = [pr[nb]], [pi[nb]]
    for _ in range(levels - 1):
        r, i = a_re[-1], a_im[-1]
        a_re.append(r * r - i * i)
        a_im.append(2.0 * r * i)
    a_re, a_im = jnp.stack(a_re, axis=0), jnp.stack(a_im, axis=0)
    a_re = a_re.reshape(levels, tiles, -1).transpose(1, 0, 2)
    a_im = a_im.reshape(levels, tiles, -1).transpose(1, 0, 2)
    return c0.astype(BF16), pc.astype(BF16), qc.astype(BF16), a_re, a_im, d_skip.astype(F32).reshape(1, g * c)


def _k5_kernel(ys_ref, h_ref, p_ref, w1_ref, w2_ref, nffn_ref, up_hbm, down_hbm, nple_ref, gate_ref,
               proj_ref, nfinal_ref, o_ref, perm_ref, *ffn_scratch):
    first_step = pl.program_id(0) == 0
    ffn = (up_hbm, down_hbm) + ffn_scratch
    _ffn_prefetch(first_step, ffn)
    blocks = ys_ref.shape[1]
    for s in range(S5_BLOCK):
        for c, lanes in enumerate(_lane_chunks(D_MODEL)):
            perm_ref[c, pl.ds(s, blocks, stride=PERM_PITCH), :] = ys_ref[s, :, lanes].astype(F32)
    yg = jnp.concatenate(
        [jnp.concatenate([perm_ref[c, PERM_PITCH * m:PERM_PITCH * m + S5_BLOCK, :] for m in range(blocks)], axis=0)
         for c in range(D_MODEL // 128)], axis=1).astype(BF16)
    h = h_ref[...] + _dot(yg, w1_ref[...]) * _sigmoid(_dot(yg, w2_ref[...]))
    h = _tail(h, first_step, p_ref, nffn_ref, ffn, nple_ref, gate_ref, proj_ref)
    o_ref[...] =_rms(h, nfinal_ref[...])


def _k5(ys, h, p3d, w1, w2, nffn, up, down, nple, gate, proj, nfinal):
    t = h.shape[0]
    rows = min(TAIL_ROWS, t)
    consts = [w1, w2, nffn, up, down, nple, gate, proj, nfinal]
    return pl.pallas_call(
        _k5_kernel,
        grid=(t // rows,),
        in_specs=[_posmajor_spec(rows), _row_spec(rows, D_MODEL), _layer_spec(rows, 1)]
                 + [_HBM_SPEC if c is up or c is down else _const_spec(c.shape) for c in consts],
        out_specs=_row_spec(rows, D_MODEL),
        out_shape=jax.ShapeDtypeStruct((t, D_MODEL), F32),
        scratch_shapes=[pltpu.VMEM((D_MODEL // 128, PERM_PITCH * (rows // S5_BLOCK), 128), F32)] + _ffn_scratch(),
        compiler_params=_params("arbitrary"),
        name="k5_glu_mlp_ple_norm",
    )(ys, h, p3d, *consts)


def _row(vec):
    return vec.astype(F32).reshape(1, -1)


def _lora_rows(m, first, width):
    return jnp.zeros((D_LORA, width), F32).at[first:first + m.shape[0]].set(m.astype(F32)).astype(BF16)


def kernel(x, p, l0_norm_mix, l0_w_in, l0_shift_mu, l0_w0, l0_w_lora_up, l0_a0, l0_a_lora_up, l0_g_lora_up, l0_k_k, l0_k_a, l0_r_k, l0_ln_w, l0_ln_b, l0_conv_w, l0_w_out, l0_norm_ffn, l0_ffn_up, l0_ffn_down, l0_norm_ple, l0_ple_gate, l0_ple_proj, l1_norm_mix, l1_lambda_re, l1_lambda_im, l1_log_step, l1_b_re, l1_b_im, l1_c_re, l1_c_im, l1_d_skip, l1_glu_w1, l1_glu_w2, l1_norm_ffn, l1_ffn_up, l1_ffn_down, l1_norm_ple, l1_ple_gate, l1_ple_proj, norm_final):
    bsz, t, _ = x.shape
    assert bsz == 1 and t % K1_ROWS == 0 and t % TAIL_ROWS == 0
    x2d = x.reshape(t, D_MODEL)
    p3d = p.reshape(p.shape[0], t, D_PLE)

    at, rt, v, btk, bgk, gam, g, bv, yb = _k1(
        x2d, _row(l0_norm_mix), l0_w_in.astype(BF16), _row(l0_shift_mu[:D_IN_A]), _row(l0_w0),
        _lora_rows(l0_w_lora_up, 0, D_RWKV), _row(l0_a0), _lora_rows(l0_a_lora_up, LORA_W, D_RWKV),
        _lora_rows(l0_g_lora_up, LORA_W + LORA_A, D_RWKV), _row(l0_k_k), _row(l0_k_a), _row(l0_r_k),
        l0_conv_w.astype(F32))
    y = _k2(at, rt, v, btk, bgk, gam)
    h, hn = _k3(y, g, bv, yb, x2d, p3d, _row(l0_ln_w), _row(l0_ln_b), l0_w_out.astype(BF16),
                _row(l0_norm_ffn), l0_ffn_up.astype(F32), l0_ffn_down.astype(F32),
                _row(l0_norm_ple), l0_ple_gate.astype(BF16), l0_ple_proj.astype(BF16), _row(l1_norm_mix))

    tables = _s5_tables(l1_lambda_re, l1_lambda_im, l1_log_step, l1_b_re, l1_b_im, l1_c_re, l1_c_im,
                        l1_d_skip, t // S5_BLOCK)
    ys = _k4(hn, *tables)

    out = _k5(ys, h, p3d, l1_glu_w1.astype(BF16), l1_glu_w2.astype(BF16), _row(l1_norm_ffn),
              l1_ffn_up.astype(F32), l1_ffn_down.astype(F32), _row(l1_norm_ple),
              l1_ple_gate.astype(BF16), l1_ple_proj.astype(BF16), _row(norm_final))
    return out.reshape(bsz, t, D_MODEL)
```

```python
import numpy as np
import jax
import jax.numpy as jnp
from jax import lax
from jax.experimental import pallas as pl
from jax.experimental.pallas import tpu as pltpu

F32 = jnp.float32
BF16 = jnp.bfloat16

D_MODEL = 1024
N_HEADS = 8
HEAD = 64
D_RWKV = N_HEADS * HEAD
D_CONV = D_MODEL - D_RWKV
LORA_W, LORA_A, LORA_G = 64, 64, 128
D_LORA = LORA_W + LORA_A + LORA_G
D_IN_A = 3 * D_RWKV + D_LORA
D_IN = D_IN_A + 3 * D_CONV
GN_EPS = 64e-5
RMS_EPS = 1e-6
SSM_GROUPS, SSM_GROUP, SSM_STATE = 64, 16, 64
D_FF = 4 * D_MODEL
D_PLE = 256

VMEM_LIMIT = 60 * 1024 * 1024

RWKV_CHUNK = 64
K1_ROWS = 512
K1_CHUNKS = K1_ROWS // RWKV_CHUNK
K2_CHUNKS = 8
TAIL_ROWS = 512
FF_CHUNK = 1024
FF_STAGE = 512
S5_BLOCK = 16
S5_ROWS = 512
PERM_PITCH = S5_BLOCK + 1


def _dot(a, b):
    return jnp.dot(a, b, preferred_element_type=F32)


def _dot_nt(a, b):
    return lax.dot_general(a, b, (((1,), (1,)), ((), ())), preferred_element_type=F32)


def _split_bf16(x, n):
    pieces = []
    for _ in range(n - 1):
        p = x.astype(BF16)
        pieces.append(p)
        x = x - p.astype(F32)
    pieces.append(x.astype(BF16))
    return pieces


def _rms(x, gain):
    return x * lax.rsqrt(jnp.mean(x * x, axis=-1, keepdims=True) + RMS_EPS) * gain


def _sigmoid(x):
    return 1.0 / (1.0 + jnp.exp(-x))


def _softplus(x):
    return jnp.maximum(x, 0.0) + jnp.log(1.0 + jnp.exp(-jnp.abs(x)))


def _gelu_tanh(x):
    return 0.5 * x * (1.0 + jnp.tanh(np.sqrt(2.0 / np.pi).astype(np.float32) * (x + 0.044715 * (x * x * x))))


def _const_spec(shape):
    nd = len(shape)
    return pl.BlockSpec(shape, lambda *_: (0,) * nd, pipeline_mode=pl.Buffered(1))


def _row_spec(rows, cols):
    return pl.BlockSpec((rows, cols), lambda i: (i, 0))


def _layer_spec(rows, layer):
    return pl.BlockSpec((None, rows, D_PLE), lambda i: (layer, i, 0))


def _posmajor_spec(rows):
    return pl.BlockSpec((S5_BLOCK, rows // S5_BLOCK, D_MODEL), lambda i: (0, i, 0))


def _lane_chunks(width):
    return [slice(c, c + 128) for c in range(0, width, 128)]


def _params(*sem):
    return pltpu.CompilerParams(dimension_semantics=sem, vmem_limit_bytes=VMEM_LIMIT)


def _head_ones(scale):
    lane = np.arange(D_RWKV)
    return jnp.asarray(scale * ((lane[:, None] // HEAD) == (lane[None, :] // HEAD)), BF16)


def _k1_kernel(x_ref, gmix_ref, win_ref, mu_ref, w0_ref, ww_ref, a0_ref, wa_ref, wg_ref, kk_ref, ka_ref,
               rk_ref, cw_ref, tri_ref, sel_ref, hsum_ref,
               at_ref, rt_ref, v_ref, btk_ref, bgk_ref, gam_ref, g_ref, bv_ref, yb_ref,
               zs_ref, us_ref):
    rows = x_ref.shape[0]

    @pl.when(pl.program_id(0) == 0)
    def _():
        zs_ref[0:8, :] = jnp.zeros((8, D_IN_A), F32)
        us_ref[0:8, :] = jnp.zeros((8, D_CONV), F32)

    xn = _rms(x_ref[...], gmix_ref[...])
    z = _dot(xn.astype(BF16), win_ref[...])

    za = z[:, :D_IN_A]
    zs_ref[8:8 + rows, :] = za
    za_prev = zs_ref[7:7 + rows, :]
    zs_ref[7:8, :] = za[rows - 1:rows, :]
    za = za + mu_ref[...] * (za_prev - za)

    r = za[:, 0:D_RWKV]
    k = za[:, D_RWKV:2 * D_RWKV]
    v = za[:, 2 * D_RWKV:3 * D_RWKV]
    lora_in = za[:, 3 * D_RWKV:]

    w_log = -_softplus(-(w0_ref[...] + _dot(jnp.tanh(lora_in).astype(BF16), ww_ref[...]))) - 0.5
    logw = -jnp.exp(w_log)
    lr = _sigmoid(a0_ref[...] + _dot(lora_in.astype(BF16), wa_ref[...]))
    g_ref[...] = _dot(_sigmoid(lora_in).astype(BF16), wg_ref[...])

    lw = _split_bf16(logw, 3)
    c_incl = sum(_dot(tri_ref[...], p) for p in lw[:2])
    c_chunk = sum(_dot(sel_ref[...], p) for p in lw)
    gam_ref[...] = jnp.exp(c_chunk)
    chunk_id = lax.broadcasted_iota(jnp.int32, (rows, 1), 0) // RWKV_CHUNK
    c_tot = c_chunk[0:1, :]
    for i in range(1, rows // RWKV_CHUNK):
        c_tot = jnp.where(chunk_id == i, c_chunk[i:i + 1, :], c_tot)
    e_neg = jnp.exp(-c_incl)
    e_rest = jnp.exp(c_tot - c_incl)

    kk = k * kk_ref[...]
    kk = kk / jnp.maximum(jnp.sqrt(_dot((kk * kk).astype(BF16), hsum_ref[...])), 1e-12)
    kmod = k * (1.0 + (lr - 1.0) * ka_ref[...])
    at_ref[...] = (-kk * jnp.exp(c_incl - logw)).astype(BF16)
    rt_ref[...] = (r * jnp.exp(c_incl)).astype(BF16)

    def chunk_major_t(parts):
        n = RWKV_CHUNK
        stacked = [p[c * n:(c + 1) * n] for c in range(rows // n) for p in parts]
        return jnp.concatenate(stacked, axis=0).T.astype(BF16)

    b_t, k_t = kk * lr * e_neg, kmod * e_neg
    btk_ref[...] = chunk_major_t([b_t, b_t, k_t, k_t])
    bgk_ref[...] = chunk_major_t([kk * lr * e_rest, kmod * e_rest])
    v_ref[...] = v.astype(BF16)
    bv_ref[...] = _dot((r * kmod * rk_ref[...]).astype(BF16), hsum_ref[...]) * v

    zb = z[:, D_IN_A:]
    b_gate = zb[:, :D_CONV]
    u = zb[:, D_CONV:2 * D_CONV] * zb[:, 2 * D_CONV:]
    us_ref[8:8 + rows, :] = u
    u1 = us_ref[7:7 + rows, :]
    u2 = us_ref[6:6 + rows, :]
    us_ref[6:8, :] = u[rows - 2:rows, :]
    cw = cw_ref[...]
    yb_ref[...] = (b_gate * (cw[0:1] * u + cw[1:2] * u1 + cw[2:3] * u2)).astype(BF16)


def _k1(x2d, gmix, win, mu, w0, ww, a0, wa, wg, k_k, k_a, r_k, conv_w):
    t = x2d.shape[0]
    rows = K1_ROWS
    r_idx = np.arange(rows)
    same = (r_idx[:, None] // RWKV_CHUNK) == (r_idx[None, :] // RWKV_CHUNK)
    tri = jnp.asarray(same & (r_idx[None, :] <= r_idx[:, None]), BF16)
    sel = jnp.asarray(np.arange(K1_CHUNKS)[:, None] == (r_idx[None, :] // RWKV_CHUNK), BF16)

    consts = [gmix, win, mu, w0, ww, a0, wa, wg, k_k, k_a, r_k, conv_w, tri, sel, _head_ones(1.0)]
    wide = jax.ShapeDtypeStruct((t, D_RWKV), BF16)
    out_shape = [wide] * 3 + [
        jax.ShapeDtypeStruct((D_RWKV, 4 * t), BF16),
        jax.ShapeDtypeStruct((D_RWKV, 2 * t), BF16),
        jax.ShapeDtypeStruct((t // rows, K1_CHUNKS, D_RWKV), F32),
        jax.ShapeDtypeStruct((t, D_RWKV), F32),
        jax.ShapeDtypeStruct((t, D_RWKV), F32),
        jax.ShapeDtypeStruct((t, D_CONV), BF16),
    ]
    out_specs = [_row_spec(rows, D_RWKV)] * 3 + [
        pl.BlockSpec((D_RWKV, 4 * rows), lambda i: (0, i)), pl.BlockSpec((D_RWKV, 2 * rows), lambda i: (0, i)),
        pl.BlockSpec((None, K1_CHUNKS, D_RWKV), lambda i: (i, 0, 0)),
        _row_spec(rows, D_RWKV), _row_spec(rows, D_RWKV), _row_spec(rows, D_CONV)]
    return pl.pallas_call(
        _k1_kernel,
        grid=(t // rows,),
        in_specs=[_row_spec(rows, D_MODEL)] + [_const_spec(c.shape) for c in consts],
        out_specs=out_specs,
        out_shape=out_shape,
        scratch_shapes=[pltpu.VMEM((rows + 8, D_IN_A), F32), pltpu.VMEM((rows + 8, D_CONV), F32)],
        compiler_params=_params("arbitrary"),
        name="k1_inproj_prep",
    )(x2d, *consts)


def _k2_kernel(at_ref, rt_ref, v_ref, btk_ref, bgk_ref, gam_ref, y_ref, s_ref):
    n = RWKV_CHUNK
    pw = 2 * HEAD
    step = pl.program_id(0)

    @pl.when(step == 0)
    def _():
        s_ref[...] = jnp.zeros_like(s_ref)

    zero = jnp.zeros((), BF16)
    iota = lambda shape, d: lax.broadcasted_iota(jnp.int32, shape, d)
    same = iota((pw, pw), 0) // HEAD == iota((pw, pw), 1) // HEAD
    same_wide = jnp.concatenate([same, same], axis=1)
    diag = iota((pw, pw), 0) == iota((pw, pw), 1)
    eye2 = jnp.where(iota((n, pw), 0) == iota((n, pw), 1) % n, 1.0, 0.0).astype(F32)
    row2, col2 = iota((2 * n, 2 * pw), 0), iota((2 * n, 2 * pw), 1) % n
    tri2 = col2 <= jnp.where(row2 < n, row2 - 1, row2 - n)
    sub = iota((K1_CHUNKS, D_RWKV), 0)
    first = (step * K2_CHUNKS) % (K1_ROWS // n)

    def bd(m):
        return jnp.where(same, jnp.concatenate([m, m], axis=0), zero)

    units = [(c, p) for c in range(K2_CHUNKS) for p in range(N_HEADS // 2)]
    tile = lambda ref, u: ref[u[0] * n:(u[0] + 1) * n, u[1] * pw:(u[1] + 1) * pw]
    a = {u: tile(at_ref, u) for u in units}
    r = {u: tile(rt_ref, u) for u in units}
    v = {u: tile(v_ref, u) for u in units}
    gm = {}
    for u in units:
        c, p = u
        rhs = jnp.where(same_wide, btk_ref[p * pw:(p + 1) * pw, c * 4 * n:(c + 1) * 4 * n], zero)
        gm[u] = jnp.where(tri2, _dot(jnp.concatenate([a[u], r[u]], axis=0), rhs), 0.0)
    gmb = {u: gm[u].astype(BF16) for u in units}
    wv = {u: _dot(gmb[u][:, pw:], bd(v[u])) for u in units}

    q = {u: gm[u][:n, :pw] for u in units}
    tinv = {u: eye2 + q[u] for u in units}
    for u in units:
        qb = q[u].astype(BF16)
        q[u] = _dot(qb, bd(qb))
    for level in range(1, 6):
        for u in units:
            qb = q[u].astype(BF16)
            if level < 5:
                tq = _dot(jnp.concatenate([tinv[u].astype(BF16), qb], axis=0), bd(qb))
                tinv[u], q[u] = tinv[u] + tq[:n], tq[n:]
            else:
                tinv[u] = tinv[u] + _dot(tinv[u].astype(BF16), bd(qb))

    au, ry, mc = {}, {}, {}
    for u in units:
        rhs = jnp.concatenate([bd(a[u]), bd(wv[u][:n].astype(BF16))], axis=1)
        au[u] = _dot(tinv[u].astype(BF16), rhs).astype(BF16)
    for u in units:
        rhs = jnp.concatenate([bd(au[u][:, :pw]), bd(au[u][:, pw:])], axis=1)
        ry[u] = jnp.concatenate([r[u].astype(F32), wv[u][n:]], axis=1) + _dot(gmb[u][n:, :pw], rhs)
    for u in units:
        c, p = u
        lhs = bgk_ref[p * pw:(p + 1) * pw, c * 2 * n:(c + 1) * 2 * n]
        rhs = jnp.concatenate([au[u], jnp.concatenate([jnp.zeros((n, pw), BF16), v[u]], axis=1)], axis=0)
        mc[u] = jnp.where(same_wide, _dot(lhs, rhs), 0.0)
    for u in units:
        c, p = u
        gam_row = jnp.sum(jnp.where(sub == first + c, gam_ref[...], 0.0), axis=0, keepdims=True)
        m_t = jnp.where(diag, gam_row[:, p * pw:(p + 1) * pw], 0.0) + mc[u][:, :pw]
        mr = _dot(jnp.concatenate([m_t, ry[u][:, :pw]], axis=0).astype(BF16), s_ref[p].astype(BF16))
        s_ref[p] = mr[:pw] + mc[u][:, pw:]
        y_ref[c * n:(c + 1) * n, p * pw:(p + 1) * pw] = mr[pw:] + ry[u][:, pw:]


def _k2(at, rt, v, btk, bgk, gam):
    t = at.shape[0]
    rows = RWKV_CHUNK * K2_CHUNKS
    per = K1_ROWS // rows
    seq = _row_spec(rows, D_RWKV)
    return pl.pallas_call(
        _k2_kernel,
        grid=(t // rows,),
        in_specs=[seq] * 3 + [pl.BlockSpec((D_RWKV, 4 * rows), lambda i: (0, i)),
                              pl.BlockSpec((D_RWKV, 2 * rows), lambda i: (0, i)),
                              pl.BlockSpec((None, K1_CHUNKS, D_RWKV), lambda i: (i // per, 0, 0))],
        out_specs=seq,
        out_shape=jax.ShapeDtypeStruct((t, D_RWKV), F32),
        scratch_shapes=[pltpu.VMEM((N_HEADS // 2, 2 * HEAD, 2 * HEAD), F32)],
        compiler_params=_params("arbitrary"),
        name="k2_rwkv_chunked",
    )(at, rt, v, btk, bgk, gam)


_HBM_SPEC = pl.BlockSpec(memory_space=pltpu.HBM)


def _ffn_scratch():
    return [pltpu.VMEM((D_MODEL, D_FF), BF16), pltpu.VMEM((D_FF, D_MODEL), BF16),
            pltpu.VMEM((2, D_MODEL, FF_STAGE), F32), pltpu.VMEM((2, FF_STAGE, D_MODEL), F32),
            pltpu.SemaphoreType.DMA((2, 2))]


def _ffn_copies(c, ffn):
    up_hbm, down_hbm, _, _, stage_up, stage_down, sem = ffn
    cols = pl.ds(c * FF_STAGE, FF_STAGE)
    return (pltpu.make_async_copy(up_hbm.at[:, cols], stage_up.at[c % 2], sem.at[0, c % 2]),
            pltpu.make_async_copy(down_hbm.at[cols, :], stage_down.at[c % 2], sem.at[1, c % 2]))


def _ffn_load(first_step, ffn):
    @pl.when(first_step)
    def _():
        _, _, up_ref, down_ref, stage_up, stage_down, _ = ffn
        chunks = D_FF // FF_STAGE
        for c in range(min(2, chunks)):
            for copy in _ffn_copies(c, ffn):
                copy.start()
        for c in range(chunks):
            cols = slice(c * FF_STAGE, (c + 1) * FF_STAGE)
            for copy in _ffn_copies(c, ffn):
                copy.wait()
            up_ref[:, cols] = stage_up[c % 2].astype(BF16)
            down_ref[cols, :] = stage_down[c % 2].astype(BF16)
            if c + 2 < chunks:
                for copy in _ffn_copies(c + 2, ffn):
                    copy.start()


def _tail(h, p_ref, nffn_ref, ffn, nple_ref, gate_ref, proj_ref):
    _, _, up_ref, down_ref, _, _, _ = ffn
    hn = _rms(h, nffn_ref[...]).astype(BF16)
    mlp = None
    for c in range(D_FF // FF_CHUNK):
        cols = slice(c * FF_CHUNK, (c + 1) * FF_CHUNK)
        a = jnp.maximum(_dot(hn, up_ref[:, cols]), 0.0)
        part = _dot((a * a).astype(BF16), down_ref[cols, :])
        mlp = part if mlp is None else mlp + part
    h = h + mlp
    gate = _sigmoid(_dot(_rms(h, nple_ref[...]).astype(BF16), gate_ref[...]))
    return h + _dot(p_ref[...].astype(BF16), proj_ref[...]) * gate


def _k3_kernel(y_ref, g_ref, bv_ref, yb_ref, x_ref, p_ref, gnmean_ref, lnw_ref, lnb_ref, wout_ref,
               nffn_ref, up_hbm, down_hbm, nple_ref, gate_ref, proj_ref, nnext_ref, h_ref, hn_ref, perm_ref,
               *ffn_scratch):
    ffn = (up_hbm, down_hbm) + ffn_scratch
    _ffn_load(pl.program_id(0) == 0, ffn)
    y = y_ref[...]
    d = y - _dot(y.astype(BF16), gnmean_ref[...])
    var = _dot((d * d).astype(BF16), gnmean_ref[...])
    yn = d * lax.rsqrt(var + GN_EPS) * lnw_ref[...] + lnb_ref[...]
    ya = ((yn + bv_ref[...]) * g_ref[...]).astype(BF16)
    h = x_ref[...] + _dot(ya, wout_ref[0:D_RWKV, :]) + _dot(yb_ref[...], wout_ref[D_RWKV:, :])
    h = _tail(h, p_ref, nffn_ref, ffn, nple_ref, gate_ref, proj_ref)
    h_ref[...] = h
    hn = _rms(h, nnext_ref[...])
    blocks = hn.shape[0] // S5_BLOCK
    for c, lanes in enumerate(_lane_chunks(D_MODEL)):
        for m in range(blocks):
            perm_ref[c, PERM_PITCH * m:PERM_PITCH * m + S5_BLOCK, :] = hn[m * S5_BLOCK:(m + 1) * S5_BLOCK, lanes]
    for s in range(S5_BLOCK):
        for c, lanes in enumerate(_lane_chunks(D_MODEL)):
            hn_ref[s, :, lanes] = perm_ref[c, pl.ds(s, blocks, stride=PERM_PITCH), :].astype(BF16)


def _k3(y, g, bv, yb, x2d, p3d, ln_w, ln_b, wout, nffn, up, down, nple, gate, proj, nnext):
    t = x2d.shape[0]
    rows = min(TAIL_ROWS, t)
    consts = [_head_ones(1.0 / HEAD), ln_w, ln_b, wout, nffn, up, down, nple, gate, proj, nnext]
    return pl.pallas_call(
        _k3_kernel,
        grid=(t // rows,),
        in_specs=[_row_spec(rows, D_RWKV), _row_spec(rows, D_RWKV), _row_spec(rows, D_RWKV),
                  _row_spec(rows, D_CONV), _row_spec(rows, D_MODEL), _layer_spec(rows, 0)]
                 + [_HBM_SPEC if c is up or c is down else _const_spec(c.shape) for c in consts],
        out_specs=[_row_spec(rows, D_MODEL), _posmajor_spec(rows)],
        out_shape=[jax.ShapeDtypeStruct((t, D_MODEL), F32),
                   jax.ShapeDtypeStruct((S5_BLOCK, t // S5_BLOCK, D_MODEL), BF16)],
        scratch_shapes=[pltpu.VMEM((D_MODEL // 128, PERM_PITCH * (rows // S5_BLOCK), 128), F32)] + _ffn_scratch(),
        compiler_params=_params("arbitrary"),
        name="k3_mix_out_mlp_ple",
    )(y, g, bv, yb, x2d, p3d, *consts)


def _shift_rows(x, s):
    n = x.shape[0]
    if s % 8 == 0:
        return jnp.concatenate([jnp.zeros((s, x.shape[1]), x.dtype), x[:n - s]], axis=0)
    keep = lax.broadcasted_iota(jnp.int32, x.shape, 0) >= s
    return jnp.where(keep, pltpu.roll(x, s, axis=0), 0.0)


def _k4_kernel(x_ref, c0_ref, pc_ref, qc_ref, are_ref, aim_ref, dsk_ref, y_ref, wt_ref, carry_ref):
    nb, rows, lanes = x_ref.shape
    sw = 2 * SSM_STATE
    zero = jnp.zeros((), BF16)

    pw = 2 * SSM_GROUP
    npair = lanes // pw

    @pl.when(pl.program_id(1) == 0)
    def _():
        carry_ref[...] = jnp.zeros_like(carry_ref)
        b_same = ((lax.broadcasted_iota(jnp.int32, (nb * pw, 2 * sw), 0) // SSM_GROUP) % 2
                  == lax.broadcasted_iota(jnp.int32, (nb * pw, 2 * sw), 1) // sw)
        c_group = lax.broadcasted_iota(jnp.int32, (2 * sw, lanes), 1) // SSM_GROUP
        c_half = lax.broadcasted_iota(jnp.int32, (2 * sw, lanes), 0) // sw
        c_two = jnp.concatenate([c0_ref[...]] * 2, axis=0)
        for j in range(npair):
            lam_b = jnp.where(b_same, jnp.concatenate([pc_ref[j]] * 2, axis=1), zero)
            taps = _dot(lam_b, jnp.where(c_group == 2 * j + c_half, c_two, zero)).astype(BF16)
            for s in range(nb):
                wt_ref[s * lanes + j * pw:s * lanes + (j + 1) * pw, :] = taps[s * pw:(s + 1) * pw, :]

    slot = lax.broadcasted_iota(jnp.int32, (1, 128), 1) // pw

    def gather(src_of, src_slot_of, count):
        cols = []
        for w in range(count // 4):
            col = None
            for kk in range(4):
                src = src_of(4 * w + kk)
                shift = (pw * (kk - src_slot_of(4 * w + kk))) % 128
                src = pltpu.roll(src, shift, axis=1) if shift else src
                col = src if col is None else jnp.where(slot == kk, src, col)
            cols.append(col)
        return jnp.concatenate(cols, axis=1)

    ns = SSM_STATE
    p_same = ((lax.broadcasted_iota(jnp.int32, (nb * pw, 2 * ns), 0) // SSM_GROUP) % 2
              == lax.broadcasted_iota(jnp.int32, (nb * pw, 2 * ns), 1) // ns)
    parts_r, parts_i = [], []
    for j in range(npair):
        lhs = gather(lambda s: x_ref[s][:, 128 * (j // 4):128 * (j // 4 + 1)], lambda s: j % 4, nb)
        pcj = pc_ref[j]
        parts_r.append(_dot(lhs, jnp.where(p_same, jnp.concatenate([pcj[:, :ns]] * 2, axis=1), zero)))
        parts_i.append(_dot(lhs, jnp.where(p_same, jnp.concatenate([pcj[:, ns:]] * 2, axis=1), zero)))
    sr = jnp.concatenate(parts_r, axis=1)
    si = jnp.concatenate(parts_i, axis=1)

    first = lax.broadcasted_iota(jnp.int32, sr.shape, 0) == 0
    hr_in, hi_in = carry_ref[0:1, :], carry_ref[8:9, :]
    ar, ai = are_ref[0:1, :], aim_ref[0:1, :]
    sr = sr + jnp.where(first, ar * hr_in - ai * hi_in, 0.0)
    si = si + jnp.where(first, ar * hi_in + ai * hr_in, 0.0)
    for k in range(are_ref.shape[0]):
        if (1 << k) >= rows:
            break
        ar, ai = are_ref[k:k + 1, :], aim_ref[k:k + 1, :]
        tr, ti = _shift_rows(sr, 1 << k), _shift_rows(si, 1 << k)
        sr, si = sr + ar * tr - ai * ti, si + ar * ti + ai * tr
    carry_ref[0:1, :] = sr[rows - 1:rows, :]
    carry_ref[8:9, :] = si[rows - 1:rows, :]
    hbr = jnp.where(first, hr_in, _shift_rows(sr, 1)).astype(BF16)
    hbi = jnp.where(first, hi_in, _shift_rows(si, 1)).astype(BF16)

    y_in = []
    for j in range(npair):
        qcj = qc_ref[j]
        q_re = jnp.where(p_same, jnp.concatenate([qcj[:, :ns]] * 2, axis=1), zero)
        q_im = jnp.where(p_same, jnp.concatenate([qcj[:, ns:]] * 2, axis=1), zero)
        cols = slice(j * 2 * ns, (j + 1) * 2 * ns)
        y_in.append((_dot_nt(hbr[:, cols], q_re) + _dot_nt(hbi[:, cols], q_im)).astype(BF16))
    dsk = dsk_ref[...]
    for t in range(nb):
        xcat = jnp.concatenate([x_ref[s] for s in range(t + 1)], axis=1)
        y_t = gather(lambda j: y_in[j][:, 128 * (t // 4):128 * (t // 4 + 1)], lambda j: t % 4, npair)
        y = _dot(xcat, wt_ref[(nb - 1 - t) * lanes:, :]) + y_t.astype(F32)
        y_ref[t] = _gelu_tanh(y + dsk * x_ref[t].astype(F32)).astype(BF16)


def _k4(xs, c0, pc, qc, a_re, a_im, dsk):
    nb, m, d = xs.shape
    lanes = 256
    rows = min(S5_ROWS, m)
    return pl.pallas_call(
        _k4_kernel,
        grid=(d // lanes, m // rows),
        in_specs=[pl.BlockSpec((nb, rows, lanes), lambda q, i: (0, i, q)),
                  pl.BlockSpec((None,) + c0.shape[1:], lambda q, i: (q, 0, 0)),
                  pl.BlockSpec((None,) + pc.shape[1:], lambda q, i: (q, 0, 0, 0)),
                  pl.BlockSpec((None,) + qc.shape[1:], lambda q, i: (q, 0, 0, 0)),
                  pl.BlockSpec((None,) + a_re.shape[1:], lambda q, i: (q, 0, 0)),
                  pl.BlockSpec((None,) + a_im.shape[1:], lambda q, i: (q, 0, 0)),
                  pl.BlockSpec((1, lanes), lambda q, i: (0, q))],
        out_specs=pl.BlockSpec((nb, rows, lanes), lambda q, i: (0, i, q)),
        out_shape=jax.ShapeDtypeStruct((nb, m, d), BF16),
        scratch_shapes=[pltpu.VMEM((nb * lanes, lanes), BF16),
                        pltpu.VMEM((16, SSM_STATE * (lanes // SSM_GROUP)), F32)],
        compiler_params=_params("arbitrary", "arbitrary"),
        name="k4_s5",
    )(xs, c0, pc, qc, a_re, a_im, dsk)


def _s5_tables(lam_re, lam_im, log_step, b_re, b_im, c_re, c_im, d_skip, n_blocks):
    g, p, c, nb = SSM_GROUPS, SSM_STATE, SSM_GROUP, S5_BLOCK
    lre = jnp.minimum(lam_re.astype(F32), -1e-4)
    lim = lam_im.astype(F32)
    step = jnp.exp(log_step.astype(F32))[:, None]
    ar, ai = lre * step, lim * step
    n = jnp.arange(nb + 1, dtype=F32)[:, None, None]
    mag = jnp.exp(n * ar)
    pr, pi = mag * jnp.cos(n * ai), mag * jnp.sin(n * ai)
    nr, ni = pr[1] - 1.0, pi[1]
    den = lre * lre + lim * lim
    qr, qi = (nr * lre + ni * lim) / den, (ni * lre - nr * lim) / den
    bre, bim = b_re.astype(F32), b_im.astype(F32)
    bbr = qr[..., None] * bre - qi[..., None] * bim
    bbi = qr[..., None] * bim + qi[..., None] * bre
    cre, cim = c_re.astype(F32), c_im.astype(F32)
    tiles, pairs = g * c // 256, 256 // (2 * c)
    split = lambda m: m.reshape((tiles, pairs, 2) + m.shape[1:])
    both = lambda a, b: jnp.concatenate([a, b], axis=-1)
    bt_r, bt_i = bbr.transpose(0, 2, 1), bbi.transpose(0, 2, 1)
    b_for_re = split(both(bt_r, bt_i))[:, :, None]
    b_for_im = split(both(-bt_i, bt_r))[:, :, None]
    ps_r, ps_i = pr[nb - 1::-1].transpose(1, 0, 2), pi[nb - 1::-1].transpose(1, 0, 2)
    ps_r = split(both(ps_r, ps_r)).transpose(0, 1, 3, 2, 4)[..., None, :]
    ps_i = split(both(ps_i, ps_i)).transpose(0, 1, 3, 2, 4)[..., None, :]
    pc = (ps_r * b_for_re + ps_i * b_for_im).reshape(tiles, pairs, nb * 2 * c, 2 * p)
    c_for_re = split(both(cre, -cim))[:, :, None]
    c_for_im = split(both(-cim, -cre))[:, :, None]
    pt_r, pt_i = pr[1:].transpose(1, 0, 2), pi[1:].transpose(1, 0, 2)
    pt_r = split(both(pt_r, pt_r)).transpose(0, 1, 3, 2, 4)[..., None, :]
    pt_i = split(both(pt_i, pt_i)).transpose(0, 1, 3, 2, 4)[..., None, :]
    qc = (pt_r * c_for_re + pt_i * c_for_im).reshape(tiles, pairs, nb * 2 * c, 2 * p)
    ct_r, ct_i = cre.transpose(0, 2, 1), cim.transpose(0, 2, 1)
    c0 = jnp.stack([ct_r, -ct_i], axis=1).reshape(tiles, 256 // c, 2 * p, c)
    c0 = c0.transpose(0, 2, 1, 3).reshape(tiles, 2 * p, 256)
    levels = max(1, int(np.ceil(np.log2(max(n_blocks, 2)))))
    a_re, a_im = [pr[nb]], [pi[nb]]
    for _ in range(levels - 1):
        r, i = a_re[-1], a_im[-1]
        a_re.append(r * r - i * i)
        a_im.append(2.0 * r * i)
    a_re, a_im = jnp.stack(a_re, axis=0), jnp.stack(a_im, axis=0)
    a_re = a_re.reshape(levels, tiles, -1).transpose(1, 0, 2)
    a_im = a_im.reshape(levels, tiles, -1).transpose(1, 0, 2)
    return c0.astype(BF16), pc.astype(BF16), qc.astype(BF16), a_re, a_im, d_skip.astype(F32).reshape(1, g * c)


def _k5_kernel(ys_ref, h_ref, p_ref, w1_ref, w2_ref, nffn_ref, up_hbm, down_hbm, nple_ref, gate_ref,
               proj_ref, nfinal_ref, o_ref, perm_ref, *ffn_scratch):
    ffn = (up_hbm, down_hbm) + ffn_scratch
    _ffn_load(pl.program_id(0) == 0, ffn)
    blocks = ys_ref.shape[1]
    for s in range(S5_BLOCK):
        for c, lanes in enumerate(_lane_chunks(D_MODEL)):
            perm_ref[c, pl.ds(s, blocks, stride=PERM_PITCH), :] = ys_ref[s, :, lanes].astype(F32)
    yg = jnp.concatenate(
        [jnp.concatenate([perm_ref[c, PERM_PITCH * m:PERM_PITCH * m + S5_BLOCK, :] for m in range(blocks)], axis=0)
         for c in range(D_MODEL // 128)], axis=1).astype(BF16)
    h = h_ref[...] + _dot(yg, w1_ref[...]) * _sigmoid(_dot(yg, w2_ref[...]))
    h = _tail(h, p_ref, nffn_ref, ffn, nple_ref, gate_ref, proj_ref)
    o_ref[...] =_rms(h, nfinal_ref[...])


def _k5(ys, h, p3d, w1, w2, nffn, up, down, nple, gate, proj, nfinal):
    t = h.shape[0]
    rows = min(TAIL_ROWS, t)
    consts = [w1, w2, nffn, up, down, nple, gate, proj, nfinal]
    return pl.pallas_call(
        _k5_kernel,
        grid=(t // rows,),
        in_specs=[_posmajor_spec(rows), _row_spec(rows, D_MODEL), _layer_spec(rows, 1)]
                 + [_HBM_SPEC if c is up or c is down else _const_spec(c.shape) for c in consts],
        out_specs=_row_spec(rows, D_MODEL),
        out_shape=jax.ShapeDtypeStruct((t, D_MODEL), F32),
        scratch_shapes=[pltpu.VMEM((D_MODEL // 128, PERM_PITCH * (rows // S5_BLOCK), 128), F32)] + _ffn_scratch(),
        compiler_params=_params("arbitrary"),
        name="k5_glu_mlp_ple_norm",
    )(ys, h, p3d, *consts)


def _row(vec):
    return vec.astype(F32).reshape(1, -1)


def _lora_rows(m, first, width):
    return jnp.zeros((D_LORA, width), F32).at[first:first + m.shape[0]].set(m.astype(F32)).astype(BF16)


def kernel(x, p, l0_norm_mix, l0_w_in, l0_shift_mu, l0_w0, l0_w_lora_up, l0_a0, l0_a_lora_up, l0_g_lora_up, l0_k_k, l0_k_a, l0_r_k, l0_ln_w, l0_ln_b, l0_conv_w, l0_w_out, l0_norm_ffn, l0_ffn_up, l0_ffn_down, l0_norm_ple, l0_ple_gate, l0_ple_proj, l1_norm_mix, l1_lambda_re, l1_lambda_im, l1_log_step, l1_b_re, l1_b_im, l1_c_re, l1_c_im, l1_d_skip, l1_glu_w1, l1_glu_w2, l1_norm_ffn, l1_ffn_up, l1_ffn_down, l1_norm_ple, l1_ple_gate, l1_ple_proj, norm_final):
    bsz, t, _ = x.shape
    assert bsz == 1 and t % K1_ROWS == 0 and t % TAIL_ROWS == 0
    x2d = x.reshape(t, D_MODEL)
    p3d = p.reshape(p.shape[0], t, D_PLE)

    at, rt, v, btk, bgk, gam, g, bv, yb = _k1(
        x2d, _row(l0_norm_mix), l0_w_in.astype(BF16), _row(l0_shift_mu[:D_IN_A]), _row(l0_w0),
        _lora_rows(l0_w_lora_up, 0, D_RWKV), _row(l0_a0), _lora_rows(l0_a_lora_up, LORA_W, D_RWKV),
        _lora_rows(l0_g_lora_up, LORA_W + LORA_A, D_RWKV), _row(l0_k_k), _row(l0_k_a), _row(l0_r_k),
        l0_conv_w.astype(F32))
    y = _k2(at, rt, v, btk, bgk, gam)
    h, hn = _k3(y, g, bv, yb, x2d, p3d, _row(l0_ln_w), _row(l0_ln_b), l0_w_out.astype(BF16),
                _row(l0_norm_ffn), l0_ffn_up.astype(F32), l0_ffn_down.astype(F32),
                _row(l0_norm_ple), l0_ple_gate.astype(BF16), l0_ple_proj.astype(BF16), _row(l1_norm_mix))

    tables = _s5_tables(l1_lambda_re, l1_lambda_im, l1_log_step, l1_b_re, l1_b_im, l1_c_re, l1_c_im,
                        l1_d_skip, t // S5_BLOCK)
    ys = _k4(hn, *tables)

    out = _k5(ys, h, p3d, l1_glu_w1.astype(BF16), l1_glu_w2.astype(BF16), _row(l1_norm_ffn),
              l1_ffn_up.astype(F32), l1_ffn_down.astype(F32), _row(l1_norm_ple),
              l1_ple_gate.astype(BF16), l1_ple_proj.astype(BF16), _row(norm_final))
    return out.reshape(bsz, t, D_MODEL)
```

```python
import numpy as np
import jax
import jax.numpy as jnp
from jax import lax
from jax.experimental import pallas as pl
from jax.experimental.pallas import tpu as pltpu

F32 = jnp.float32
BF16 = jnp.bfloat16

D_MODEL = 1024
N_HEADS = 8
HEAD = 64
D_RWKV = N_HEADS * HEAD
D_CONV = D_MODEL - D_RWKV
LORA_W, LORA_A, LORA_G = 64, 64, 128
D_LORA = LORA_W + LORA_A + LORA_G
D_IN_A = 3 * D_RWKV + D_LORA
D_IN = D_IN_A + 3 * D_CONV
GN_EPS = 64e-5
RMS_EPS = 1e-6
SSM_GROUPS, SSM_GROUP, SSM_STATE = 64, 16, 64
D_FF = 4 * D_MODEL
D_PLE = 256

VMEM_LIMIT = 60 * 1024 * 1024

RWKV_CHUNK = 64
K1_ROWS = 512
K1_CHUNKS = K1_ROWS // RWKV_CHUNK
K2_CHUNKS = 8
TAIL_ROWS = 512
FF_CHUNK = 1024
FF_STAGE = 512
WIN_STAGE = 256
S5_BLOCK = 16
S5_ROWS = 512
PERM_PITCH = S5_BLOCK + 1


def _dot(a, b):
    return jnp.dot(a, b, preferred_element_type=F32)


def _dot_nt(a, b):
    return lax.dot_general(a, b, (((1,), (1,)), ((), ())), preferred_element_type=F32)


def _split_bf16(x, n):
    pieces = []
    for _ in range(n - 1):
        p = x.astype(BF16)
        pieces.append(p)
        x = x - p.astype(F32)
    pieces.append(x.astype(BF16))
    return pieces


def _rms(x, gain):
    return x * lax.rsqrt(jnp.mean(x * x, axis=-1, keepdims=True) + RMS_EPS) * gain


def _sigmoid(x):
    return 1.0 / (1.0 + jnp.exp(-x))


def _softplus(x):
    return jnp.maximum(x, 0.0) + jnp.log(1.0 + jnp.exp(-jnp.abs(x)))


def _gelu_tanh(x):
    return 0.5 * x * (1.0 + jnp.tanh(np.sqrt(2.0 / np.pi).astype(np.float32) * (x + 0.044715 * (x * x * x))))


def _const_spec(shape):
    nd = len(shape)
    return pl.BlockSpec(shape, lambda *_: (0,) * nd, pipeline_mode=pl.Buffered(1))


def _row_spec(rows, cols):
    return pl.BlockSpec((rows, cols), lambda i: (i, 0))


def _layer_spec(rows, layer):
    return pl.BlockSpec((None, rows, D_PLE), lambda i: (layer, i, 0))


def _posmajor_spec(rows):
    return pl.BlockSpec((S5_BLOCK, rows // S5_BLOCK, D_MODEL), lambda i: (0, i, 0))


def _lane_chunks(width):
    return [slice(c, c + 128) for c in range(0, width, 128)]


def _params(*sem):
    return pltpu.CompilerParams(dimension_semantics=sem, vmem_limit_bytes=VMEM_LIMIT)


def _head_ones(scale):
    lane = np.arange(D_RWKV)
    return jnp.asarray(scale * ((lane[:, None] // HEAD) == (lane[None, :] // HEAD)), BF16)


def _k1_kernel(x_ref, gmix_ref, win_hbm, mu_ref, w0_ref, ww_ref, a0_ref, wa_ref, wg_ref, kk_ref, ka_ref,
               rk_ref, cw_ref, tri_ref, sel_ref, hsum_ref,
               at_ref, rt_ref, v_ref, btk_ref, bgk_ref, gam_ref, g_ref, bv_ref, yb_ref,
               zs_ref, us_ref, win_ref, wstage_ref, wsem):
    rows = x_ref.shape[0]

    @pl.when(pl.program_id(0) == 0)
    def _():
        zs_ref[0:8, :] = jnp.zeros((8, D_IN_A), F32)
        us_ref[0:8, :] = jnp.zeros((8, D_CONV), F32)
        chunks = D_MODEL // WIN_STAGE
        copy = lambda c: pltpu.make_async_copy(win_hbm.at[pl.ds(c * WIN_STAGE, WIN_STAGE), :],
                                               wstage_ref.at[c % 2], wsem.at[c % 2])
        for c in range(min(2, chunks)):
            copy(c).start()
        for c in range(chunks):
            copy(c).wait()
            win_ref[c * WIN_STAGE:(c + 1) * WIN_STAGE, :] = wstage_ref[c % 2].astype(BF16)
            if c + 2 < chunks:
                copy(c + 2).start()

    xn = _rms(x_ref[...], gmix_ref[...])
    z = _dot(xn.astype(BF16), win_ref[...])

    za = z[:, :D_IN_A]
    zs_ref[8:8 + rows, :] = za
    za_prev = zs_ref[7:7 + rows, :]
    zs_ref[7:8, :] = za[rows - 1:rows, :]
    za = za + mu_ref[...] * (za_prev - za)

    r = za[:, 0:D_RWKV]
    k = za[:, D_RWKV:2 * D_RWKV]
    v = za[:, 2 * D_RWKV:3 * D_RWKV]
    lora_in = za[:, 3 * D_RWKV:]

    w_log = -_softplus(-(w0_ref[...] + _dot(jnp.tanh(lora_in).astype(BF16), ww_ref[...]))) - 0.5
    logw = -jnp.exp(w_log)
    lr = _sigmoid(a0_ref[...] + _dot(lora_in.astype(BF16), wa_ref[...]))
    g_ref[...] = _dot(_sigmoid(lora_in).astype(BF16), wg_ref[...])

    lw = _split_bf16(logw, 3)
    c_incl = sum(_dot(tri_ref[...], p) for p in lw[:2])
    c_chunk = sum(_dot(sel_ref[...], p) for p in lw)
    gam_ref[...] = jnp.exp(c_chunk)
    chunk_id = lax.broadcasted_iota(jnp.int32, (rows, 1), 0) // RWKV_CHUNK
    c_tot = c_chunk[0:1, :]
    for i in range(1, rows // RWKV_CHUNK):
        c_tot = jnp.where(chunk_id == i, c_chunk[i:i + 1, :], c_tot)
    e_neg = jnp.exp(-c_incl)
    e_rest = jnp.exp(c_tot - c_incl)

    kk = k * kk_ref[...]
    kk = kk / jnp.maximum(jnp.sqrt(_dot((kk * kk).astype(BF16), hsum_ref[...])), 1e-12)
    kmod = k * (1.0 + (lr - 1.0) * ka_ref[...])
    at_ref[...] = (-kk * jnp.exp(c_incl - logw)).astype(BF16)
    rt_ref[...] = (r * jnp.exp(c_incl)).astype(BF16)

    def chunk_major_t(parts):
        n = RWKV_CHUNK
        stacked = [p[c * n:(c + 1) * n] for c in range(rows // n) for p in parts]
        return jnp.concatenate(stacked, axis=0).T.astype(BF16)

    b_t, k_t = kk * lr * e_neg, kmod * e_neg
    btk_ref[...] = chunk_major_t([b_t, b_t, k_t, k_t])
    bgk_ref[...] = chunk_major_t([kk * lr * e_rest, kmod * e_rest])
    v_ref[...] = v.astype(BF16)
    bv_ref[...] = _dot((r * kmod * rk_ref[...]).astype(BF16), hsum_ref[...]) * v

    zb = z[:, D_IN_A:]
    b_gate = zb[:, :D_CONV]
    u = zb[:, D_CONV:2 * D_CONV] * zb[:, 2 * D_CONV:]
    us_ref[8:8 + rows, :] = u
    u1 = us_ref[7:7 + rows, :]
    u2 = us_ref[6:6 + rows, :]
    us_ref[6:8, :] = u[rows - 2:rows, :]
    cw = cw_ref[...]
    yb_ref[...] = (b_gate * (cw[0:1] * u + cw[1:2] * u1 + cw[2:3] * u2)).astype(BF16)


def _k1(x2d, gmix, win, mu, w0, ww, a0, wa, wg, k_k, k_a, r_k, conv_w):
    t = x2d.shape[0]
    rows = K1_ROWS
    r_idx = np.arange(rows)
    same = (r_idx[:, None] // RWKV_CHUNK) == (r_idx[None, :] // RWKV_CHUNK)
    tri = jnp.asarray(same & (r_idx[None, :] <= r_idx[:, None]), BF16)
    sel = jnp.asarray(np.arange(K1_CHUNKS)[:, None] == (r_idx[None, :] // RWKV_CHUNK), BF16)

    consts = [gmix, win, mu, w0, ww, a0, wa, wg, k_k, k_a, r_k, conv_w, tri, sel, _head_ones(1.0)]
    wide = jax.ShapeDtypeStruct((t, D_RWKV), BF16)
    out_shape = [wide] * 3 + [
        jax.ShapeDtypeStruct((D_RWKV, 4 * t), BF16),
        jax.ShapeDtypeStruct((D_RWKV, 2 * t), BF16),
        jax.ShapeDtypeStruct((t // rows, K1_CHUNKS, D_RWKV), F32),
        jax.ShapeDtypeStruct((t, D_RWKV), F32),
        jax.ShapeDtypeStruct((t, D_RWKV), F32),
        jax.ShapeDtypeStruct((t, D_CONV), BF16),
    ]
    out_specs = [_row_spec(rows, D_RWKV)] * 3 + [
        pl.BlockSpec((D_RWKV, 4 * rows), lambda i: (0, i)), pl.BlockSpec((D_RWKV, 2 * rows), lambda i: (0, i)),
        pl.BlockSpec((None, K1_CHUNKS, D_RWKV), lambda i: (i, 0, 0)),
        _row_spec(rows, D_RWKV), _row_spec(rows, D_RWKV), _row_spec(rows, D_CONV)]
    return pl.pallas_call(
        _k1_kernel,
        grid=(t // rows,),
        in_specs=[_row_spec(rows, D_MODEL)] + [_HBM_SPEC if c is win else _const_spec(c.shape) for c in consts],
        out_specs=out_specs,
        out_shape=out_shape,
        scratch_shapes=[pltpu.VMEM((rows + 8, D_IN_A), F32), pltpu.VMEM((rows + 8, D_CONV), F32),
                        pltpu.VMEM((D_MODEL, D_IN), BF16), pltpu.VMEM((2, WIN_STAGE, D_IN), F32),
                        pltpu.SemaphoreType.DMA((2,))],
        compiler_params=_params("arbitrary"),
        name="k1_inproj_prep",
    )(x2d, *consts)


def _k2_kernel(at_ref, rt_ref, v_ref, btk_ref, bgk_ref, gam_ref, y_ref, s_ref):
    n = RWKV_CHUNK
    pw = 2 * HEAD
    step = pl.program_id(0)

    @pl.when(step == 0)
    def _():
        s_ref[...] = jnp.zeros_like(s_ref)

    zero = jnp.zeros((), BF16)
    iota = lambda shape, d: lax.broadcasted_iota(jnp.int32, shape, d)
    same = iota((pw, pw), 0) // HEAD == iota((pw, pw), 1) // HEAD
    same_wide = jnp.concatenate([same, same], axis=1)
    diag = iota((pw, pw), 0) == iota((pw, pw), 1)
    eye2 = jnp.where(iota((n, pw), 0) == iota((n, pw), 1) % n, 1.0, 0.0).astype(F32)
    row2, col2 = iota((2 * n, 2 * pw), 0), iota((2 * n, 2 * pw), 1) % n
    tri2 = col2 <= jnp.where(row2 < n, row2 - 1, row2 - n)
    sub = iota((K1_CHUNKS, D_RWKV), 0)
    first = (step * K2_CHUNKS) % (K1_ROWS // n)

    def bd(m):
        return jnp.where(same, jnp.concatenate([m, m], axis=0), zero)

    units = [(c, p) for c in range(K2_CHUNKS) for p in range(N_HEADS // 2)]
    tile = lambda ref, u: ref[u[0] * n:(u[0] + 1) * n, u[1] * pw:(u[1] + 1) * pw]
    a = {u: tile(at_ref, u) for u in units}
    r = {u: tile(rt_ref, u) for u in units}
    v = {u: tile(v_ref, u) for u in units}
    gm = {}
    for u in units:
        c, p = u
        rhs = jnp.where(same_wide, btk_ref[p * pw:(p + 1) * pw, c * 4 * n:(c + 1) * 4 * n], zero)
        gm[u] = jnp.where(tri2, _dot(jnp.concatenate([a[u], r[u]], axis=0), rhs), 0.0)
    gmb = {u: gm[u].astype(BF16) for u in units}
    wv = {u: _dot(gmb[u][:, pw:], bd(v[u])) for u in units}

    q = {u: gm[u][:n, :pw] for u in units}
    tinv = {u: eye2 + q[u] for u in units}
    for u in units:
        qb = q[u].astype(BF16)
        q[u] = _dot(qb, bd(qb))
    for level in range(1, 6):
        for u in units:
            qb = q[u].astype(BF16)
            if level < 5:
                tq = _dot(jnp.concatenate([tinv[u].astype(BF16), qb], axis=0), bd(qb))
                tinv[u], q[u] = tinv[u] + tq[:n], tq[n:]
            else:
                tinv[u] = tinv[u] + _dot(tinv[u].astype(BF16), bd(qb))

    au, ry, mc = {}, {}, {}
    for u in units:
        rhs = jnp.concatenate([bd(a[u]), bd(wv[u][:n].astype(BF16))], axis=1)
        au[u] = _dot(tinv[u].astype(BF16), rhs).astype(BF16)
    for u in units:
        rhs = jnp.concatenate([bd(au[u][:, :pw]), bd(au[u][:, pw:])], axis=1)
        ry[u] = jnp.concatenate([r[u].astype(F32), wv[u][n:]], axis=1) + _dot(gmb[u][n:, :pw], rhs)
    for u in units:
        c, p = u
        lhs = bgk_ref[p * pw:(p + 1) * pw, c * 2 * n:(c + 1) * 2 * n]
        rhs = jnp.concatenate([au[u], jnp.concatenate([jnp.zeros((n, pw), BF16), v[u]], axis=1)], axis=0)
        mc[u] = jnp.where(same_wide, _dot(lhs, rhs), 0.0)
    for u in units:
        c, p = u
        gam_row = jnp.sum(jnp.where(sub == first + c, gam_ref[...], 0.0), axis=0, keepdims=True)
        m_t = jnp.where(diag, gam_row[:, p * pw:(p + 1) * pw], 0.0) + mc[u][:, :pw]
        mr = _dot(jnp.concatenate([m_t, ry[u][:, :pw]], axis=0).astype(BF16), s_ref[p].astype(BF16))
        s_ref[p] = mr[:pw] + mc[u][:, pw:]
        y_ref[c * n:(c + 1) * n, p * pw:(p + 1) * pw] = mr[pw:] + ry[u][:, pw:]


def _k2(at, rt, v, btk, bgk, gam):
    t = at.shape[0]
    rows = RWKV_CHUNK * K2_CHUNKS
    per = K1_ROWS // rows
    seq = _row_spec(rows, D_RWKV)
    return pl.pallas_call(
        _k2_kernel,
        grid=(t // rows,),
        in_specs=[seq] * 3 + [pl.BlockSpec((D_RWKV, 4 * rows), lambda i: (0, i)),
                              pl.BlockSpec((D_RWKV, 2 * rows), lambda i: (0, i)),
                              pl.BlockSpec((None, K1_CHUNKS, D_RWKV), lambda i: (i // per, 0, 0))],
        out_specs=seq,
        out_shape=jax.ShapeDtypeStruct((t, D_RWKV), F32),
        scratch_shapes=[pltpu.VMEM((N_HEADS // 2, 2 * HEAD, 2 * HEAD), F32)],
        compiler_params=_params("arbitrary"),
        name="k2_rwkv_chunked",
    )(at, rt, v, btk, bgk, gam)


_HBM_SPEC = pl.BlockSpec(memory_space=pltpu.HBM)


def _ffn_scratch():
    return [pltpu.VMEM((D_MODEL, D_FF), BF16), pltpu.VMEM((D_FF, D_MODEL), BF16),
            pltpu.VMEM((2, D_MODEL, FF_STAGE), F32), pltpu.VMEM((2, FF_STAGE, D_MODEL), F32),
            pltpu.SemaphoreType.DMA((2, 2))]


def _ffn_copies(c, ffn):
    up_hbm, down_hbm, _, _, stage_up, stage_down, sem = ffn
    cols = pl.ds(c * FF_STAGE, FF_STAGE)
    return (pltpu.make_async_copy(up_hbm.at[:, cols], stage_up.at[c % 2], sem.at[0, c % 2]),
            pltpu.make_async_copy(down_hbm.at[cols, :], stage_down.at[c % 2], sem.at[1, c % 2]))


def _ffn_load(first_step, ffn):
    @pl.when(first_step)
    def _():
        _, _, up_ref, down_ref, stage_up, stage_down, _ = ffn
        chunks = D_FF // FF_STAGE
        for c in range(min(2, chunks)):
            for copy in _ffn_copies(c, ffn):
                copy.start()
        for c in range(chunks):
            cols = slice(c * FF_STAGE, (c + 1) * FF_STAGE)
            for copy in _ffn_copies(c, ffn):
                copy.wait()
            up_ref[:, cols] = stage_up[c % 2].astype(BF16)
            down_ref[cols, :] = stage_down[c % 2].astype(BF16)
            if c + 2 < chunks:
                for copy in _ffn_copies(c + 2, ffn):
                    copy.start()


def _tail(h, p_ref, nffn_ref, ffn, nple_ref, gate_ref, proj_ref):
    _, _, up_ref, down_ref, _, _, _ = ffn
    hn = _rms(h, nffn_ref[...]).astype(BF16)
    mlp = None
    for c in range(D_FF // FF_CHUNK):
        cols = slice(c * FF_CHUNK, (c + 1) * FF_CHUNK)
        a = jnp.maximum(_dot(hn, up_ref[:, cols]), 0.0)
        part = _dot((a * a).astype(BF16), down_ref[cols, :])
        mlp = part if mlp is None else mlp + part
    h = h + mlp
    gate = _sigmoid(_dot(_rms(h, nple_ref[...]).astype(BF16), gate_ref[...]))
    return h + _dot(p_ref[...].astype(BF16), proj_ref[...]) * gate


def _k3_kernel(y_ref, g_ref, bv_ref, yb_ref, x_ref, p_ref, gnmean_ref, lnw_ref, lnb_ref, wout_ref,
               nffn_ref, up_hbm, down_hbm, nple_ref, gate_ref, proj_ref, nnext_ref, h_ref, hn_ref, perm_ref,
               *ffn_scratch):
    ffn = (up_hbm, down_hbm) + ffn_scratch
    _ffn_load(pl.program_id(0) == 0, ffn)
    y = y_ref[...]
    d = y - _dot(y.astype(BF16), gnmean_ref[...])
    var = _dot((d * d).astype(BF16), gnmean_ref[...])
    yn = d * lax.rsqrt(var + GN_EPS) * lnw_ref[...] + lnb_ref[...]
    ya = ((yn + bv_ref[...]) * g_ref[...]).astype(BF16)
    h = x_ref[...] + _dot(ya, wout_ref[0:D_RWKV, :]) + _dot(yb_ref[...], wout_ref[D_RWKV:, :])
    h = _tail(h, p_ref, nffn_ref, ffn, nple_ref, gate_ref, proj_ref)
    h_ref[...] = h
    hn = _rms(h, nnext_ref[...])
    blocks = hn.shape[0] // S5_BLOCK
    for c, lanes in enumerate(_lane_chunks(D_MODEL)):
        for m in range(blocks):
            perm_ref[c, PERM_PITCH * m:PERM_PITCH * m + S5_BLOCK, :] = hn[m * S5_BLOCK:(m + 1) * S5_BLOCK, lanes]
    for s in range(S5_BLOCK):
        for c, lanes in enumerate(_lane_chunks(D_MODEL)):
            hn_ref[s, :, lanes] = perm_ref[c, pl.ds(s, blocks, stride=PERM_PITCH), :].astype(BF16)


def _k3(y, g, bv, yb, x2d, p3d, ln_w, ln_b, wout, nffn, up, down, nple, gate, proj, nnext):
    t = x2d.shape[0]
    rows = min(TAIL_ROWS, t)
    consts = [_head_ones(1.0 / HEAD), ln_w, ln_b, wout, nffn, up, down, nple, gate, proj, nnext]
    return pl.pallas_call(
        _k3_kernel,
        grid=(t // rows,),
        in_specs=[_row_spec(rows, D_RWKV), _row_spec(rows, D_RWKV), _row_spec(rows, D_RWKV),
                  _row_spec(rows, D_CONV), _row_spec(rows, D_MODEL), _layer_spec(rows, 0)]
                 + [_HBM_SPEC if c is up or c is down else _const_spec(c.shape) for c in consts],
        out_specs=[_row_spec(rows, D_MODEL), _posmajor_spec(rows)],
        out_shape=[jax.ShapeDtypeStruct((t, D_MODEL), F32),
                   jax.ShapeDtypeStruct((S5_BLOCK, t // S5_BLOCK, D_MODEL), BF16)],
        scratch_shapes=[pltpu.VMEM((D_MODEL // 128, PERM_PITCH * (rows // S5_BLOCK), 128), F32)] + _ffn_scratch(),
        compiler_params=_params("arbitrary"),
        name="k3_mix_out_mlp_ple",
    )(y, g, bv, yb, x2d, p3d, *consts)


def _shift_rows(x, s):
    n = x.shape[0]
    if s % 8 == 0:
        return jnp.concatenate([jnp.zeros((s, x.shape[1]), x.dtype), x[:n - s]], axis=0)
    keep = lax.broadcasted_iota(jnp.int32, x.shape, 0) >= s
    return jnp.where(keep, pltpu.roll(x, s, axis=0), 0.0)


def _k4_kernel(x_ref, c0_ref, pc_ref, qc_ref, are_ref, aim_ref, dsk_ref, y_ref, wt_ref, carry_ref):
    nb, rows, lanes = x_ref.shape
    sw = 2 * SSM_STATE
    zero = jnp.zeros((), BF16)

    pw = 2 * SSM_GROUP
    npair = lanes // pw

    @pl.when(pl.program_id(1) == 0)
    def _():
        carry_ref[...] = jnp.zeros_like(carry_ref)
        b_same = ((lax.broadcasted_iota(jnp.int32, (nb * pw, 2 * sw), 0) // SSM_GROUP) % 2
                  == lax.broadcasted_iota(jnp.int32, (nb * pw, 2 * sw), 1) // sw)
        c_group = lax.broadcasted_iota(jnp.int32, (2 * sw, lanes), 1) // SSM_GROUP
        c_half = lax.broadcasted_iota(jnp.int32, (2 * sw, lanes), 0) // sw
        c_two = jnp.concatenate([c0_ref[...]] * 2, axis=0)
        for j in range(npair):
            lam_b = jnp.where(b_same, jnp.concatenate([pc_ref[j]] * 2, axis=1), zero)
            taps = _dot(lam_b, jnp.where(c_group == 2 * j + c_half, c_two, zero)).astype(BF16)
            for s in range(nb):
                wt_ref[s * lanes + j * pw:s * lanes + (j + 1) * pw, :] = taps[s * pw:(s + 1) * pw, :]

    slot = lax.broadcasted_iota(jnp.int32, (1, 128), 1) // pw

    def gather(src_of, src_slot_of, count):
        cols = []
        for w in range(count // 4):
            col = None
            for kk in range(4):
                src = src_of(4 * w + kk)
                shift = (pw * (kk - src_slot_of(4 * w + kk))) % 128
                src = pltpu.roll(src, shift, axis=1) if shift else src
                col = src if col is None else jnp.where(slot == kk, src, col)
            cols.append(col)
        return jnp.concatenate(cols, axis=1)

    ns = SSM_STATE
    p_same = ((lax.broadcasted_iota(jnp.int32, (nb * pw, 2 * ns), 0) // SSM_GROUP) % 2
              == lax.broadcasted_iota(jnp.int32, (nb * pw, 2 * ns), 1) // ns)
    parts_r, parts_i = [], []
    for j in range(npair):
        lhs = gather(lambda s: x_ref[s][:, 128 * (j // 4):128 * (j // 4 + 1)], lambda s: j % 4, nb)
        pcj = pc_ref[j]
        parts_r.append(_dot(lhs, jnp.where(p_same, jnp.concatenate([pcj[:, :ns]] * 2, axis=1), zero)))
        parts_i.append(_dot(lhs, jnp.where(p_same, jnp.concatenate([pcj[:, ns:]] * 2, axis=1), zero)))
    sr = jnp.concatenate(parts_r, axis=1)
    si = jnp.concatenate(parts_i, axis=1)

    first = lax.broadcasted_iota(jnp.int32, sr.shape, 0) == 0
    hr_in, hi_in = carry_ref[0:1, :], carry_ref[8:9, :]
    ar, ai = are_ref[0:1, :], aim_ref[0:1, :]
    sr = sr + jnp.where(first, ar * hr_in - ai * hi_in, 0.0)
    si = si + jnp.where(first, ar * hi_in + ai * hr_in, 0.0)
    for k in range(are_ref.shape[0]):
        if (1 << k) >= rows:
            break
        ar, ai = are_ref[k:k + 1, :], aim_ref[k:k + 1, :]
        tr, ti = _shift_rows(sr, 1 << k), _shift_rows(si, 1 << k)
        sr, si = sr + ar * tr - ai * ti, si + ar * ti + ai * tr
    carry_ref[0:1, :] = sr[rows - 1:rows, :]
    carry_ref[8:9, :] = si[rows - 1:rows, :]
    hbr = jnp.where(first, hr_in, _shift_rows(sr, 1)).astype(BF16)
    hbi = jnp.where(first, hi_in, _shift_rows(si, 1)).astype(BF16)

    y_in = []
    for j in range(npair):
        qcj = qc_ref[j]
        q_re = jnp.where(p_same, jnp.concatenate([qcj[:, :ns]] * 2, axis=1), zero)
        q_im = jnp.where(p_same, jnp.concatenate([qcj[:, ns:]] * 2, axis=1), zero)
        cols = slice(j * 2 * ns, (j + 1) * 2 * ns)
        y_in.append((_dot_nt(hbr[:, cols], q_re) + _dot_nt(hbi[:, cols], q_im)).astype(BF16))
    dsk = dsk_ref[...]
    for t in range(nb):
        xcat = jnp.concatenate([x_ref[s] for s in range(t + 1)], axis=1)
        y_t = gather(lambda j: y_in[j][:, 128 * (t // 4):128 * (t // 4 + 1)], lambda j: t % 4, npair)
        y = _dot(xcat, wt_ref[(nb - 1 - t) * lanes:, :]) + y_t.astype(F32)
        y_ref[t] = _gelu_tanh(y + dsk * x_ref[t].astype(F32)).astype(BF16)


def _k4(xs, c0, pc, qc, a_re, a_im, dsk):
    nb, m, d = xs.shape
    lanes = 256
    rows = min(S5_ROWS, m)
    return pl.pallas_call(
        _k4_kernel,
        grid=(d // lanes, m // rows),
        in_specs=[pl.BlockSpec((nb, rows, lanes), lambda q, i: (0, i, q)),
                  pl.BlockSpec((None,) + c0.shape[1:], lambda q, i: (q, 0, 0)),
                  pl.BlockSpec((None,) + pc.shape[1:], lambda q, i: (q, 0, 0, 0)),
                  pl.BlockSpec((None,) + qc.shape[1:], lambda q, i: (q, 0, 0, 0)),
                  pl.BlockSpec((None,) + a_re.shape[1:], lambda q, i: (q, 0, 0)),
                  pl.BlockSpec((None,) + a_im.shape[1:], lambda q, i: (q, 0, 0)),
                  pl.BlockSpec((1, lanes), lambda q, i: (0, q))],
        out_specs=pl.BlockSpec((nb, rows, lanes), lambda q, i: (0, i, q)),
        out_shape=jax.ShapeDtypeStruct((nb, m, d), BF16),
        scratch_shapes=[pltpu.VMEM((nb * lanes, lanes), BF16),
                        pltpu.VMEM((16, SSM_STATE * (lanes // SSM_GROUP)), F32)],
        compiler_params=_params("arbitrary", "arbitrary"),
        name="k4_s5",
    )(xs, c0, pc, qc, a_re, a_im, dsk)


def _s5_tables(lam_re, lam_im, log_step, b_re, b_im, c_re, c_im, d_skip, n_blocks):
    g, p, c, nb = SSM_GROUPS, SSM_STATE, SSM_GROUP, S5_BLOCK
    lre = jnp.minimum(lam_re.astype(F32), -1e-4)
    lim = lam_im.astype(F32)
    step = jnp.exp(log_step.astype(F32))[:, None]
    ar, ai = lre * step, lim * step
    n = jnp.arange(nb + 1, dtype=F32)[:, None, None]
    mag = jnp.exp(n * ar)
    pr, pi = mag * jnp.cos(n * ai), mag * jnp.sin(n * ai)
    nr, ni = pr[1] - 1.0, pi[1]
    den = lre * lre + lim * lim
    qr, qi = (nr * lre + ni * lim) / den, (ni * lre - nr * lim) / den
    bre, bim = b_re.astype(F32), b_im.astype(F32)
    bbr = qr[..., None] * bre - qi[..., None] * bim
    bbi = qr[..., None] * bim + qi[..., None] * bre
    cre, cim = c_re.astype(F32), c_im.astype(F32)
    tiles, pairs = g * c // 256, 256 // (2 * c)
    split = lambda m: m.reshape((tiles, pairs, 2) + m.shape[1:])
    both = lambda a, b: jnp.concatenate([a, b], axis=-1)
    bt_r, bt_i = bbr.transpose(0, 2, 1), bbi.transpose(0, 2, 1)
    b_for_re = split(both(bt_r, bt_i))[:, :, None]
    b_for_im = split(both(-bt_i, bt_r))[:, :, None]
    ps_r, ps_i = pr[nb - 1::-1].transpose(1, 0, 2), pi[nb - 1::-1].transpose(1, 0, 2)
    ps_r = split(both(ps_r, ps_r)).transpose(0, 1, 3, 2, 4)[..., None, :]
    ps_i = split(both(ps_i, ps_i)).transpose(0, 1, 3, 2, 4)[..., None, :]
    pc = (ps_r * b_for_re + ps_i * b_for_im).reshape(tiles, pairs, nb * 2 * c, 2 * p)
    c_for_re = split(both(cre, -cim))[:, :, None]
    c_for_im = split(both(-cim, -cre))[:, :, None]
    pt_r, pt_i = pr[1:].transpose(1, 0, 2), pi[1:].transpose(1, 0, 2)
    pt_r = split(both(pt_r, pt_r)).transpose(0, 1, 3, 2, 4)[..., None, :]
    pt_i = split(both(pt_i, pt_i)).transpose(0, 1, 3, 2, 4)[..., None, :]
    qc = (pt_r * c_for_re + pt_i * c_for_im).reshape(tiles, pairs, nb * 2 * c, 2 * p)
    ct_r, ct_i = cre.transpose(0, 2, 1), cim.transpose(0, 2, 1)
    c0 = jnp.stack([ct_r, -ct_i], axis=1).reshape(tiles, 256 // c, 2 * p, c)
    c0 = c0.transpose(0, 2, 1, 3).reshape(tiles, 2 * p, 256)
    levels = max(1, int(np.ceil(np.log2(max(n_blocks, 2)))))
    a_re, a_im = [pr[nb]], [pi[nb]]
    for _ in range(levels - 1):
        r, i = a_re[-1], a_im[-1]
        a_re.append(r * r - i * i)
        a_im.append(2.0 * r * i)
    a_re, a_im = jnp.stack(a_re, axis=0), jnp.stack(a_im, axis=0)
    a_re = a_re.reshape(levels, tiles, -1).transpose(1, 0, 2)
    a_im = a_im.reshape(levels, tiles, -1).transpose(1, 0, 2)
    return c0.astype(BF16), pc.astype(BF16), qc.astype(BF16), a_re, a_im, d_skip.astype(F32).reshape(1, g * c)


def _k5_kernel(ys_ref, h_ref, p_ref, w1_ref, w2_ref, nffn_ref, up_hbm, down_hbm, nple_ref, gate_ref,
               proj_ref, nfinal_ref, o_ref, perm_ref, *ffn_scratch):
    ffn = (up_hbm, down_hbm) + ffn_scratch
    _ffn_load(pl.program_id(0) == 0, ffn)
    blocks = ys_ref.shape[1]
    for s in range(S5_BLOCK):
        for c, lanes in enumerate(_lane_chunks(D_MODEL)):
            perm_ref[c, pl.ds(s, blocks, stride=PERM_PITCH), :] = ys_ref[s, :, lanes].astype(F32)
    yg = jnp.concatenate(
        [jnp.concatenate([perm_ref[c, PERM_PITCH * m:PERM_PITCH * m + S5_BLOCK, :] for m in range(blocks)], axis=0)
         for c in range(D_MODEL // 128)], axis=1).astype(BF16)
    h = h_ref[...] + _dot(yg, w1_ref[...]) * _sigmoid(_dot(yg, w2_ref[...]))
    h = _tail(h, p_ref, nffn_ref, ffn, nple_ref, gate_ref, proj_ref)
    o_ref[...] =_rms(h, nfinal_ref[...])


def _k5(ys, h, p3d, w1, w2, nffn, up, down, nple, gate, proj, nfinal):
    t = h.shape[0]
    rows = min(TAIL_ROWS, t)
    consts = [w1, w2, nffn, up, down, nple, gate, proj, nfinal]
    return pl.pallas_call(
        _k5_kernel,
        grid=(t // rows,),
        in_specs=[_posmajor_spec(rows), _row_spec(rows, D_MODEL), _layer_spec(rows, 1)]
                 + [_HBM_SPEC if c is up or c is down else _const_spec(c.shape) for c in consts],
        out_specs=_row_spec(rows, D_MODEL),
        out_shape=jax.ShapeDtypeStruct((t, D_MODEL), F32),
        scratch_shapes=[pltpu.VMEM((D_MODEL // 128, PERM_PITCH * (rows // S5_BLOCK), 128), F32)] + _ffn_scratch(),
        compiler_params=_params("arbitrary"),
        name="k5_glu_mlp_ple_norm",
    )(ys, h, p3d, *consts)


def _row(vec):
    return vec.astype(F32).reshape(1, -1)


def _lora_rows(m, first, width):
    return jnp.zeros((D_LORA, width), F32).at[first:first + m.shape[0]].set(m.astype(F32)).astype(BF16)


def kernel(x, p, l0_norm_mix, l0_w_in, l0_shift_mu, l0_w0, l0_w_lora_up, l0_a0, l0_a_lora_up, l0_g_lora_up, l0_k_k, l0_k_a, l0_r_k, l0_ln_w, l0_ln_b, l0_conv_w, l0_w_out, l0_norm_ffn, l0_ffn_up, l0_ffn_down, l0_norm_ple, l0_ple_gate, l0_ple_proj, l1_norm_mix, l1_lambda_re, l1_lambda_im, l1_log_step, l1_b_re, l1_b_im, l1_c_re, l1_c_im, l1_d_skip, l1_glu_w1, l1_glu_w2, l1_norm_ffn, l1_ffn_up, l1_ffn_down, l1_norm_ple, l1_ple_gate, l1_ple_proj, norm_final):
    bsz, t, _ = x.shape
    assert bsz == 1 and t % K1_ROWS == 0 and t % TAIL_ROWS == 0
    x2d = x.reshape(t, D_MODEL)
    p3d = p.reshape(p.shape[0], t, D_PLE)

    at, rt, v, btk, bgk, gam, g, bv, yb = _k1(
        x2d, _row(l0_norm_mix), l0_w_in.astype(F32),_row(l0_shift_mu[:D_IN_A]), _row(l0_w0),
        _lora_rows(l0_w_lora_up, 0, D_RWKV), _row(l0_a0), _lora_rows(l0_a_lora_up, LORA_W, D_RWKV),
        _lora_rows(l0_g_lora_up, LORA_W + LORA_A, D_RWKV), _row(l0_k_k), _row(l0_k_a), _row(l0_r_k),
        l0_conv_w.astype(F32))
    y = _k2(at, rt, v, btk, bgk, gam)
    h, hn = _k3(y, g, bv, yb, x2d, p3d, _row(l0_ln_w), _row(l0_ln_b), l0_w_out.astype(BF16),
                _row(l0_norm_ffn), l0_ffn_up.astype(F32), l0_ffn_down.astype(F32),
                _row(l0_norm_ple), l0_ple_gate.astype(BF16), l0_ple_proj.astype(BF16), _row(l1_norm_mix))

    tables = _s5_tables(l1_lambda_re, l1_lambda_im, l1_log_step, l1_b_re, l1_b_im, l1_c_re, l1_c_im,
                        l1_d_skip, t // S5_BLOCK)
    ys = _k4(hn, *tables)

    out = _k5(ys, h, p3d, l1_glu_w1.astype(BF16), l1_glu_w2.astype(BF16), _row(l1_norm_ffn),
              l1_ffn_up.astype(F32), l1_ffn_down.astype(F32), _row(l1_norm_ple),
              l1_ple_gate.astype(BF16), l1_ple_proj.astype(BF16), _row(norm_final))
    return out.reshape(bsz, t, D_MODEL)
```
